```python
import math
import jax, jax.numpy as jnp
from jax import lax
import numpy as np

D_MODEL = 1024
BATCH = 8
SEQ = 8192
DEPTH = 4

CTX_LEN = 256
GRID_W = 64
EPS = 1e-6
ROPE_BASE = 10000.0
Q_BLOCK = 128
N_MOD = 9
D_FF = 2816
DIFF_HEADS = D_MODEL // 256
DIFF_HD = 64
DIFF_VD = 2 * DIFF_HD
MLA_HEADS = D_MODEL // 128
MLA_NOPE = 64
MLA_ROPE = 32
MLA_V = 64
MLA_Q_RANK = 3 * D_MODEL // 8
MLA_KV_RANK = D_MODEL // 4
IN_SIZES = (DIFF_HEADS * 2 * DIFF_HD, DIFF_HEADS * 2 * DIFF_HD, DIFF_HEADS * DIFF_VD, MLA_Q_RANK, MLA_KV_RANK, MLA_ROPE)
IN_W = sum(IN_SIZES)
IN_SPLITS = tuple(sum(IN_SIZES[:i + 1]) for i in range(len(IN_SIZES) - 1))
D_MIX = DIFF_HEADS * DIFF_VD + MLA_HEADS * MLA_V
POOL_WINDOWS = (2, 4, 8, 16)
POOL_GROUPS = len(POOL_WINDOWS)
POOL_GC = D_MODEL // POOL_GROUPS

kernel_name = 'hybrid_diffattn_mla_pool_macaron_dit'


def _rms(x, g):
    x32 = x.astype(jnp.float32)
    y = x32 * lax.rsqrt(jnp.mean(x32 * x32, axis=-1, keepdims=True) + EPS)
    return y.astype(x.dtype) * g


def _modulate(x, g, mod, i):
    return _rms(x, g) * (1.0 + mod[:, :, 3 * i + 1]) + mod[:, :, 3 * i]


def _swiglu(h, wg, wu, wd):
    return (jax.nn.silu(h @ wg) * (h @ wu)) @ wd


def _ffn_half(x, g, mod, i, wg, wu, wd):
    return x + 0.5 * mod[:, :, 3 * i + 2] * _swiglu(_modulate(x, g, mod, i), wg, wu, wd)


def _rope_tables(rows, cols, dim):
    q = dim // 4
    freqs = ROPE_BASE ** (-jnp.arange(q, dtype=jnp.float32) / q)
    ang = jnp.stack([rows.astype(jnp.float32)[:, None] * freqs, cols.astype(jnp.float32)[:, None] * freqs], axis=1)
    return jnp.cos(ang), jnp.sin(ang)


def _rope(x, cs):
    cos, sin = cs
    q = x.shape[-1] // 4
    xr = x.reshape(*x.shape[:-1], 2, 2, q)
    x1, x2 = xr[..., 0, :], xr[..., 1, :]
    shape = (cos.shape[0],) + (1,) * (x.ndim - 3) + (2, q)
    cos = cos.reshape(shape).astype(x.dtype)
    sin = sin.reshape(shape).astype(x.dtype)
    return jnp.stack([x1 * cos - x2 * sin, x2 * cos + x1 * sin], axis=-2).reshape(x.shape)


def _sweep_queries(fn, q):
    B, S = q.shape[:2]
    nb = S // Q_BLOCK
    qb = jnp.moveaxis(q.reshape(B, nb, Q_BLOCK, *q.shape[2:]), 1, 0)
    out = lax.map(fn, qb)
    return jnp.moveaxis(out, 0, 1).reshape(B, S, *out.shape[3:])


def _diff_attend(q, k, v, lam):
    s = jnp.einsum('bqhgd,bkhgd->bhgqk', q, k).astype(jnp.float32) * (DIFF_HD ** -0.5)
    p = jax.nn.softmax(s, axis=-1)
    a = (p[:, :, 0] - lam * p[:, :, 1]).astype(v.dtype)
    return jnp.einsum('bhqk,bkhe->bqhe', a, v)


def _mla_attend(q, k_nope, k_rope, v):
    qn, qr = q[..., :MLA_NOPE], q[..., MLA_NOPE:]
    s = jnp.einsum('bqhd,bkhd->bhqk', qn, k_nope) + jnp.einsum('bqhd,bkd->bhqk', qr, k_rope)
    p = jax.nn.softmax(s.astype(jnp.float32) * ((MLA_NOPE + MLA_ROPE) ** -0.5), axis=-1)
    return jnp.einsum('bhqk,bkhd->bqhd', p.astype(v.dtype), v)


def _attn_project(h, aw, rope):
    w_in, qk_g, q_a_g, w_q_b, kv_a_g, w_kv_b, nope_g, rope_g = aw
    B, L, _ = h.shape
    dq, dk, dv, cq, ckv, kr = jnp.split(h @ w_in, IN_SPLITS, axis=-1)
    dq = _rms(dq.reshape(B, L, DIFF_HEADS, 2, DIFF_HD), qk_g[0])
    dk = _rms(dk.reshape(B, L, DIFF_HEADS, 2, DIFF_HD), qk_g[1])
    dv = dv.reshape(B, L, DIFF_HEADS, DIFF_VD)
    mq = (_rms(cq, q_a_g) @ w_q_b).reshape(B, L, MLA_HEADS, MLA_NOPE + MLA_ROPE)
    kv = (_rms(ckv, kv_a_g) @ w_kv_b).reshape(B, L, MLA_HEADS, MLA_NOPE + MLA_V)
    qn = _rms(mq[..., :MLA_NOPE], nope_g[0])
    qr = _rms(mq[..., MLA_NOPE:], rope_g[0])
    kn = _rms(kv[..., :MLA_NOPE], nope_g[1])
    mv = kv[..., MLA_NOPE:]
    kr = _rms(kr, rope_g[1])
    if rope is not None:
        rope_d, rope_m = rope
        dq, dk = _rope(dq, rope_d), _rope(dk, rope_d)
        qr, kr = _rope(qr, rope_m), _rope(kr, rope_m)
    return dq, dk, dv, jnp.concatenate([qn, qr], axis=-1), kn, kr, mv


def _merge(diff_o, mla_o, subln_g, lam_init, w_out):
    B, L = diff_o.shape[:2]
    d = _rms(diff_o, subln_g) * (1.0 - lam_init)
    return jnp.concatenate([d.reshape(B, L, -1), mla_o.reshape(B, L, -1)], axis=-1) @ w_out


def _attn_mixer(hl, hc, aw, lam, lam_init, subln_g, w_out, rope_l, with_ctx_queries):
    dq_l, dk_l, dv_l, mq_l, mk_l, mr_l, mv_l = _attn_project(hl, aw, rope_l)
    dq_c, dk_c, dv_c, mq_c, mk_c, mr_c, mv_c = _attn_project(hc, aw, None)
    dk = jnp.concatenate([dk_c, dk_l], axis=1)
    dv = jnp.concatenate([dv_c, dv_l], axis=1)
    mk = jnp.concatenate([mk_c, mk_l], axis=1)
    mr = jnp.concatenate([mr_c, mr_l], axis=1)
    mv = jnp.concatenate([mv_c, mv_l], axis=1)
    diff_l = _sweep_queries(lambda qb: _diff_attend(qb, dk, dv, lam), dq_l)
    mla_l = _sweep_queries(lambda qb: _mla_attend(qb, mk, mr, mv), mq_l)
    yl = _merge(diff_l, mla_l, subln_g, lam_init, w_out)
    yc = None
    if with_ctx_queries:
        diff_c = _diff_attend(dq_c, dk_c, dv_c, lam)
        mla_c = _mla_attend(mq_c, mk_c, mr_c, mv_c)
        yc = _merge(diff_c, mla_c, subln_g, lam_init, w_out)
    return yl, yc


def _pool_mixer(h, w_pool, scale):
    B, L, D = h.shape
    h32 = h.astype(jnp.float32)
    prefix = jnp.concatenate([jnp.zeros((B, 1, D), jnp.float32), jnp.cumsum(h32, axis=1)], axis=1)
    t = jnp.arange(L)
    outs = []
    for gi, w in enumerate(POOL_WINDOWS):
        sl = slice(gi * POOL_GC, (gi + 1) * POOL_GC)
        lo = jnp.clip(t - w // 2, 0, L)
        hi = jnp.clip(t + (w - w // 2), 0, L)
        cnt = (hi - lo).astype(jnp.float32)[None, :, None]
        pg = prefix[..., sl]
        outs.append((pg[:, hi] - pg[:, lo]) / cnt - h32[..., sl])
    d = jnp.stack(outs, axis=2).astype(h.dtype)
    y = jnp.einsum('blgc,gcd->blgd', d, w_pool).reshape(B, L, D)
    return y * scale


def setup_inputs(seed: int = 0) -> dict:
    key = jax.random.key(seed)
    ks = jax.random.split(key, 24)
    f32 = jnp.float32
    n_even = (DEPTH + 1) // 2
    n_odd = DEPTH // 2

    def nrm(k, shape, scale):
        return jax.random.normal(k, shape, f32) * scale

    def gain(k, shape):
        return 1.0 + 0.05 * jax.random.normal(k, shape, f32)

    return {
        'x': nrm(ks[0], (BATCH, SEQ, D_MODEL), 1.0),
        'c': nrm(ks[1], (BATCH, D_MODEL), 1.0),
        'ctx': nrm(ks[2], (BATCH, CTX_LEN, D_MODEL), 1.0),
        'c_ctx': nrm(ks[3], (D_MODEL,), 1.0),
        'mod_w': nrm(ks[4], (DEPTH, D_MODEL, N_MOD * D_MODEL), 0.5 * D_MODEL ** -0.5),
        'mod_b': nrm(ks[5], (DEPTH, N_MOD * D_MODEL), 0.01),
        'norm_g': gain(ks[6], (DEPTH, 3, D_MODEL)),
        'ffn_w_gate': nrm(ks[7], (DEPTH, 2, D_MODEL, D_FF), D_MODEL ** -0.5),
        'ffn_w_up': nrm(ks[8], (DEPTH, 2, D_MODEL, D_FF), D_MODEL ** -0.5),
        'ffn_w_down': nrm(ks[9], (DEPTH, 2, D_FF, D_MODEL), D_FF ** -0.5),
        'attn_w_in': nrm(ks[10], (n_even, D_MODEL, IN_W), D_MODEL ** -0.5),
        'diff_qk_g': gain(ks[11], (n_even, 2, DIFF_HD)),
        'diff_lambda': nrm(ks[12], (n_even, 4, DIFF_HD), 0.1),
        'diff_subln_g': gain(ks[13], (n_even, DIFF_VD)),
        'mla_q_a_g': gain(ks[14], (n_even, MLA_Q_RANK)),
        'mla_w_q_b': nrm(ks[15], (n_even, MLA_Q_RANK, MLA_HEADS * (MLA_NOPE + MLA_ROPE)), MLA_Q_RANK ** -0.5),
        'mla_kv_a_g': gain(ks[16], (n_even, MLA_KV_RANK)),
        'mla_w_kv_b': nrm(ks[17], (n_even, MLA_KV_RANK, MLA_HEADS * (MLA_NOPE + MLA_V)), MLA_KV_RANK ** -0.5),
        'mla_nope_g': gain(ks[18], (n_even, 2, MLA_NOPE)),
        'mla_rope_g': gain(ks[19], (n_even, 2, MLA_ROPE)),
        'attn_w_out': nrm(ks[20], (n_even, D_MIX, D_MODEL), D_MIX ** -0.5),
        'pool_w': nrm(ks[21], (n_odd, POOL_GROUPS, POOL_GC, POOL_GC), POOL_GC ** -0.5),
        'pool_scale': gain(ks[22], (n_odd, D_MODEL)),
    }


def reference(x, c, ctx, c_ctx, mod_w, mod_b, norm_g, ffn_w_gate, ffn_w_up, ffn_w_down,
              attn_w_in, diff_qk_g, diff_lambda, diff_subln_g, mla_q_a_g, mla_w_q_b,
              mla_kv_a_g, mla_w_kv_b, mla_nope_g, mla_rope_g, attn_w_out, pool_w, pool_scale):
    B, S, D = x.shape
    n_rows = S // GRID_W
    rows = jnp.repeat(jnp.arange(n_rows), GRID_W)
    cols = jnp.tile(jnp.arange(GRID_W), n_rows)
    rope_l = (_rope_tables(rows, cols, DIFF_HD), _rope_tables(rows, cols, MLA_ROPE))
    s_c = jax.nn.silu(c)
    s_cc = jax.nn.silu(c_ctx)[None]
    xl, xc = x, ctx
    for layer in range(DEPTH):
        even = layer % 2 == 0
        ctx_out = layer < DEPTH - 1
        ctx_in = ctx_out or even
        i = layer // 2
        g = norm_g[layer]
        fw1 = (ffn_w_gate[layer, 0], ffn_w_up[layer, 0], ffn_w_down[layer, 0])
        fw2 = (ffn_w_gate[layer, 1], ffn_w_up[layer, 1], ffn_w_down[layer, 1])
        mod_l = (s_c @ mod_w[layer] + mod_b[layer]).reshape(B, 1, N_MOD, D)
        xl = _ffn_half(xl, g[0], mod_l, 0, *fw1)
        hl = _modulate(xl, g[1], mod_l, 1)
        mod_c, hc = None, None
        if ctx_in:
            mod_c = (s_cc @ mod_w[layer] + mod_b[layer]).reshape(1, 1, N_MOD, D)
            xc = _ffn_half(xc, g[0], mod_c, 0, *fw1)
            hc = _modulate(xc, g[1], mod_c, 1)
        if even:
            lam_init = 0.8 - 0.6 * math.exp(-0.3 * layer)
            dl = diff_lambda[i].astype(jnp.float32)
            lam = jnp.exp(jnp.sum(dl[0] * dl[1])) - jnp.exp(jnp.sum(dl[2] * dl[3])) + lam_init
            aw = (attn_w_in[i], diff_qk_g[i], mla_q_a_g[i], mla_w_q_b[i], mla_kv_a_g[i],
                  mla_w_kv_b[i], mla_nope_g[i], mla_rope_g[i])
            yl, yc = _attn_mixer(hl, hc, aw, lam, lam_init, diff_subln_g[i], attn_w_out[i], rope_l, ctx_out)
        else:
            yl = _pool_mixer(hl, pool_w[i], pool_scale[i])
            yc = _pool_mixer(hc, pool_w[i], pool_scale[i]) if ctx_out else None
        xl = xl + mod_l[:, :, 5] * yl
        xl = _ffn_half(xl, g[2], mod_l, 2, *fw2)
        if ctx_out:
            xc = xc + mod_c[:, :, 5] * yc
            xc = _ffn_half(xc, g[2], mod_c, 2, *fw2)
    return xl
```

```python
import functools
import math

import jax
import jax.numpy as jnp
from jax import lax
from jax.experimental import pallas as pl
from jax.experimental.pallas import tpu as pltpu

F32 = jnp.float32
BF16 = jnp.bfloat16

EPS = 1e-6
ROPE_BASE = 10000.0
GRID_W = 64
N_MOD = 9
DIFF_HEADS = 4
DIFF_HD = 64
DIFF_VD = 128
MLA_HEADS = 8
MLA_NOPE = 64
MLA_ROPE = 32
MLA_V = 64
MLA_Q_RANK = 384
MLA_KV_RANK = 256
POOL_WINDOWS = (2, 4, 8, 16)
LANES = 128
MXU_DIM = 256
VMEM_LIMIT = 56 * 1024 * 1024

IN_DQ, IN_DK, IN_DV, IN_CQ, IN_CKV, IN_KR = 0, 512, 1024, 1536, 1920, 2176
IN_W_PAD = 2304


def _cparams(sem):
    return pltpu.CompilerParams(dimension_semantics=sem, vmem_limit_bytes=VMEM_LIMIT)


def _resident(shape):
    nd = len(shape)
    return pl.BlockSpec(shape, lambda *_: (0,) * nd, pipeline_mode=pl.Buffered(1))


def _modulate(x, g, shift, scale):
    r = lax.rsqrt(jnp.mean(x * x, axis=-1, keepdims=True) + EPS)
    return (x * r) * g * (1.0 + scale) + shift


def _mod_kernel(c_ref, w_ref, b_ref, o_ref):
    c = c_ref[...]
    s = (c * (1.0 / (1.0 + jnp.exp(-c)))).astype(BF16)
    o_ref[0] = jnp.dot(s, w_ref[0].astype(BF16), preferred_element_type=F32) + b_ref[0]


def _mod_call(cvec, mod_w, mod_b):
    depth, d, n = mod_w.shape
    rows = cvec.shape[0]
    tn = 1024
    return pl.pallas_call(
        _mod_kernel,
        grid=(depth, n // tn),
        in_specs=[
            pl.BlockSpec((rows, d), lambda l, j: (0, 0)),
            pl.BlockSpec((1, d, tn), lambda l, j: (l, 0, j)),
            pl.BlockSpec((1, 1, tn), lambda l, j: (l, 0, j)),
        ],
        out_specs=pl.BlockSpec((1, rows, tn), lambda l, j: (l, 0, j)),
        out_shape=jax.ShapeDtypeStruct((depth, rows, n), F32),
        compiler_params=_cparams(("parallel", "parallel")),
        name="mod_vectors",
    )(cvec, mod_w, mod_b.reshape(depth, 1, n))


def _ffn_body(x, mod_ref, g_ref, wg_ref, wu_ref, wd_ref, o_ref, idx, fc):
    shift = mod_ref[0, 3 * idx:3 * idx + 1, :]
    scale = mod_ref[0, 3 * idx + 1:3 * idx + 2, :]
    gate = mod_ref[0, 3 * idx + 2:3 * idx + 3, :]
    h = _modulate(x, g_ref[...], shift, scale).astype(BF16)
    d_ff = wg_ref.shape[1]
    acc = jnp.zeros(x.shape, F32)
    for c in range(d_ff // fc):
        sl = slice(c * fc, (c + 1) * fc)
        gt = jnp.dot(h, wg_ref[:, sl], preferred_element_type=F32)
        up = jnp.dot(h, wu_ref[:, sl], preferred_element_type=F32)
        a = (gt * (1.0 / (1.0 + jnp.exp(-gt))) * up).astype(BF16)
        acc = acc + jnp.dot(a, wd_ref[sl, :], preferred_element_type=F32)
    o_ref[...] = x + (0.5 * gate) * acc


def _ffn_kernel(x_ref, mod_ref, g_ref, wg_ref, wu_ref, wd_ref, o_ref, *, idx, fc):
    _ffn_body(x_ref[...], mod_ref, g_ref, wg_ref, wu_ref, wd_ref, o_ref, idx, fc)


def _ffn_attn_kernel(x_ref, u_ref, wo_ref, mod_ref, g_ref, wg_ref, wu_ref, wd_ref, o_ref, *, idx, fc):
    y = jnp.dot(u_ref[...], wo_ref[...], preferred_element_type=F32)
    x = x_ref[...] + mod_ref[0, 5:6, :] * y
    _ffn_body(x, mod_ref, g_ref, wg_ref, wu_ref, wd_ref, o_ref, idx, fc)


def _ffn_pool_kernel(x_ref, u_ref, wp_ref, ps_ref, mod_ref, g_ref, wg_ref, wu_ref, wd_ref, o_ref, *, idx, fc):
    gc = wp_ref.shape[1]
    ys = [jnp.dot(u_ref[:, k * gc:(k + 1) * gc], wp_ref[k], preferred_element_type=F32)
          for k in range(wp_ref.shape[0])]
    y = jnp.concatenate(ys, axis=-1) * ps_ref[...]
    x = x_ref[...] + mod_ref[0, 5:6, :] * y
    _ffn_body(x, mod_ref, g_ref, wg_ref, wu_ref, wd_ref, o_ref, idx, fc)


def _ffn_call(x, mod, g, wg, wu, wd, *, idx, tiles_per_mod, mod_base, tm, mixer=None):
    n, d = x.shape
    d_ff = wg.shape[1]
    fc = MXU_DIM
    tok = pl.BlockSpec((tm, d), lambda i: (i, 0))
    mod_spec = pl.BlockSpec((1, N_MOD, d), lambda i: (mod_base + i // tiles_per_mod, 0, 0))
    common_specs = [mod_spec, _resident((1, d)), _resident((d, d_ff)), _resident((d, d_ff)),
                    _resident((d_ff, d))]
    common_args = [mod, g.reshape(1, d), wg, wu, wd]
    if mixer is None:
        kern = functools.partial(_ffn_kernel, idx=idx, fc=fc)
        specs, args = [tok] + common_specs, [x] + common_args
    elif mixer[0] == "attn":
        _, u, wo = mixer
        kern = functools.partial(_ffn_attn_kernel, idx=idx, fc=fc)
        specs = [tok, pl.BlockSpec((tm, u.shape[1]), lambda i: (i, 0)), _resident(wo.shape)] + common_specs
        args = [x, u, wo] + common_args
    else:
        _, u, wp, ps = mixer
        kern = functools.partial(_ffn_pool_kernel, idx=idx, fc=fc)
        specs = [tok, pl.BlockSpec((tm, u.shape[1]), lambda i: (i, 0)), _resident(wp.shape),
                 _resident((1, d))] + common_specs
        args = [x, u, wp, ps.reshape(1, d)] + common_args
    return pl.pallas_call(
        kern,
        grid=(n // tm,),
        in_specs=specs,
        out_specs=tok,
        out_shape=jax.ShapeDtypeStruct((n, d), F32),
        compiler_params=_cparams(("parallel",)),
        name="ffn_half",
    )(*args)


def _group_rms(x, gm_ref, gain):
    outs = []
    for j in range(x.shape[1] // MXU_DIM):
        xb = x[:, j * MXU_DIM:(j + 1) * MXU_DIM]
        ms = jnp.dot((xb * xb).astype(BF16), gm_ref[...], preferred_element_type=F32)
        outs.append(xb * lax.rsqrt(ms + EPS) * gain[:, j * MXU_DIM:(j + 1) * MXU_DIM])
    return jnp.concatenate(outs, axis=-1)


def _rope(x, cos, sin, half):
    outs = []
    lane = lax.broadcasted_iota(jnp.int32, (x.shape[0], LANES), 1)
    first = (lane % (2 * half)) < half
    for j in range(x.shape[1] // LANES):
        xb = x[:, j * LANES:(j + 1) * LANES]
        partner = jnp.where(first, pltpu.roll(xb, LANES - half, 1), pltpu.roll(xb, half, 1))
        outs.append(xb * cos + partner * sin)
    return jnp.concatenate(outs, axis=-1)


def _proj_kernel(x_ref, mod_ref, g_ref, win_ref, wqb_ref, wkvb_ref, g64_ref, gm_ref,
                 gqd_ref, gkd_ref, gqa_ref, gkva_ref, gqm_ref, gkm_ref, gkr_ref,
                 cd_ref, sd_ref, cm_ref, sm_ref,
                 qd_ref, kd_ref, vd_ref, qm_ref, km_ref, vm_ref):
    x = x_ref[...]
    h = _modulate(x, g_ref[...], mod_ref[0, 3:4, :], mod_ref[0, 4:5, :]).astype(BF16)

    def inp(lo, hi):
        return jnp.dot(h, win_ref[:, lo:hi], preferred_element_type=F32)

    cd, sd, cm, sm = cd_ref[...], sd_ref[...], cm_ref[...], sm_ref[...]

    dq = _rope(_group_rms(inp(IN_DQ, IN_DK), g64_ref, gqd_ref[...]), cd, sd, DIFF_HD // 4)
    qd_ref[...] = dq.astype(BF16)
    dk = _rope(_group_rms(inp(IN_DK, IN_DV), g64_ref, gkd_ref[...]), cd, sd, DIFF_HD // 4)
    kd_ref[...] = dk.astype(BF16)
    vd_ref[...] = inp(IN_DV, IN_CQ).astype(BF16)

    cq = inp(IN_CQ, IN_CKV)
    cq = cq * lax.rsqrt(jnp.mean(cq * cq, axis=-1, keepdims=True) + EPS) * gqa_ref[...]
    mq = jnp.dot(cq.astype(BF16), wqb_ref[...], preferred_element_type=F32)
    qm_ref[...] = _rope(_group_rms(mq, gm_ref, gqm_ref[...]), cm, sm, MLA_ROPE // 4).astype(BF16)

    ckv = inp(IN_CKV, IN_KR)
    ckv = ckv * lax.rsqrt(jnp.mean(ckv * ckv, axis=-1, keepdims=True) + EPS) * gkva_ref[...]
    kv = jnp.dot(ckv.astype(BF16), wkvb_ref[...], preferred_element_type=F32)
    n_k = MLA_HEADS * LANES
    kn = _group_rms(kv[:, :n_k], gm_ref, gkm_ref[...])
    vm_ref[...] = kv[:, n_k:].astype(BF16)

    kr_raw = inp(IN_KR, IN_W_PAD)
    kr2 = jnp.concatenate([kr_raw, kr_raw], axis=-1)
    kr = _group_rms(kr2, gm_ref, jnp.concatenate([gkr_ref[...], gkr_ref[...]], axis=-1))[:, :LANES]
    kr = _rope(kr, cm, sm, MLA_ROPE // 4)
    km_ref[...] = (kn + jnp.concatenate([kr] * MLA_HEADS, axis=-1)).astype(BF16)


def _proj_call(x, mod, g, pw, tables, *, tiles_per_mod, mod_base, tiles_per_seq, tm):
    n, d = x.shape
    tok = lambda w: pl.BlockSpec((tm, w), lambda i: (i, 0))
    tab = pl.BlockSpec((tm, LANES), lambda i: (i % tiles_per_seq, 0))
    mod_spec = pl.BlockSpec((1, N_MOD, d), lambda i: (mod_base + i // tiles_per_mod, 0, 0))
    consts = [pw["w_in"], pw["w_q_b"], pw["w_kv_b"], pw["g64"], pw["gm"],
              pw["gqd"], pw["gkd"], pw["gqa"], pw["gkva"], pw["gqm"], pw["gkm"], pw["gkr"]]
    widths = (512, 512, 512, 1024, 1024, 512)
    return pl.pallas_call(
        _proj_kernel,
        grid=(n // tm,),
        in_specs=[tok(d), mod_spec, _resident((1, d))] + [_resident(c.shape) for c in consts] + [tab] * 4,
        out_specs=[tok(w) for w in widths],
        out_shape=[jax.ShapeDtypeStruct((n, w), BF16) for w in widths],
        compiler_params=_cparams(("parallel",)),
        name="attn_proj",
    )(x, mod, g.reshape(1, d), *consts, *tables)


def _online_step(s, vt, m_ref, l_ref, acc_ref, k):
    m_old = m_ref[k:k + 1, :]
    m_new = jnp.maximum(m_old, jnp.max(s, axis=0, keepdims=True))
    alpha = jnp.exp(m_old - m_new)
    p = jnp.exp(s - m_new)
    l_ref[k:k + 1, :] = alpha * l_ref[k:k + 1, :] + jnp.sum(p, axis=0, keepdims=True)
    acc_ref[k] = alpha * acc_ref[k] + jnp.dot(vt, p.astype(BF16), preferred_element_type=F32)
    m_ref[k:k + 1, :] = m_new


def _init_stats(m_ref, l_ref, acc_ref):
    m_ref[...] = jnp.full(m_ref.shape, -jnp.inf, F32)
    l_ref[...] = jnp.zeros(l_ref.shape, F32)
    acc_ref[...] = jnp.zeros(acc_ref.shape, F32)


def _diff_kernel(qt_ref, k_ref, vt_ref, lam_ref, sg_ref, o_ref, m_ref, l_ref, acc_ref, *, lam_init):
    j = pl.program_id(2)

    @pl.when(j == 0)
    def _():
        _init_stats(m_ref, l_ref, acc_ref)

    for hd in range(DIFF_HEADS):
        kpair = k_ref[0, :, hd * LANES:(hd + 1) * LANES]
        vt = vt_ref[0, hd * DIFF_VD:(hd + 1) * DIFF_VD, :]
        for sub in range(2):
            k = 2 * hd + sub
            s = jnp.dot(kpair, qt_ref[0, k * LANES:(k + 1) * LANES, :], preferred_element_type=F32)
            _online_step(s, vt, m_ref, l_ref, acc_ref, k)

    @pl.when(j == pl.num_programs(2) - 1)
    def _():
        dl = lam_ref[...]
        lam = (jnp.exp(jnp.sum(dl[0:1] * dl[1:2], keepdims=True))
               - jnp.exp(jnp.sum(dl[2:3] * dl[3:4], keepdims=True)) + lam_init)
        for hd in range(DIFF_HEADS):
            o1 = acc_ref[2 * hd] / l_ref[2 * hd:2 * hd + 1, :]
            o2 = acc_ref[2 * hd + 1] / l_ref[2 * hd + 1:2 * hd + 2, :]
            o = o1 - lam * o2
            r = lax.rsqrt(jnp.mean(o * o, axis=0, keepdims=True) + EPS)
            o = (o * r) * sg_ref[...] * (1.0 - lam_init)
            o_ref[0, hd * DIFF_VD:(hd + 1) * DIFF_VD, :] = o.astype(BF16)


def _mla_kernel(qt_ref, k_ref, vt_ref, o_ref, m_ref, l_ref, acc_ref):
    j = pl.program_id(2)

    @pl.when(j == 0)
    def _():
        _init_stats(m_ref, l_ref, acc_ref)

    for hd in range(MLA_HEADS):
        s = jnp.dot(k_ref[0, :, hd * LANES:(hd + 1) * LANES], qt_ref[0, hd * LANES:(hd + 1) * LANES, :],
                    preferred_element_type=F32)
        _online_step(s, vt_ref[0, hd * MLA_V:(hd + 1) * MLA_V, :], m_ref, l_ref, acc_ref, hd)

    @pl.when(j == pl.num_programs(2) - 1)
    def _():
        for hd in range(MLA_HEADS):
            o = acc_ref[hd] / l_ref[hd:hd + 1, :]
            o_ref[0, hd * MLA_V:(hd + 1) * MLA_V, :] = o.astype(BF16)


def _attn_specs(qt, k, vt, tq, tk):
    return [
        pl.BlockSpec((1, qt.shape[1], tq), lambda b, i, j: (b, 0, i)),
        pl.BlockSpec((1, tk, k.shape[2]), lambda b, i, j: (b, j, 0)),
        pl.BlockSpec((1, vt.shape[1], tk), lambda b, i, j: (b, 0, j)),
    ]


def _diff_call(qt, k, vt, dlam, subln_g, lam_init, *, tq, tk):
    b, _, sq = qt.shape
    sk = k.shape[1]
    n_sub = 2 * DIFF_HEADS
    return pl.pallas_call(
        functools.partial(_diff_kernel, lam_init=lam_init),
        grid=(b, sq // tq, sk // tk),
        in_specs=_attn_specs(qt, k, vt, tq, tk) + [
            pl.BlockSpec(dlam.shape, lambda b_, i, j: (0, 0)),
            pl.BlockSpec((DIFF_VD, 1), lambda b_, i, j: (0, 0)),
        ],
        out_specs=pl.BlockSpec((1, DIFF_HEADS * DIFF_VD, tq), lambda b_, i, j: (b_, 0, i)),
        out_shape=jax.ShapeDtypeStruct((b, DIFF_HEADS * DIFF_VD, sq), BF16),
        scratch_shapes=[pltpu.VMEM((n_sub, tq), F32), pltpu.VMEM((n_sub, tq), F32),
                        pltpu.VMEM((n_sub, DIFF_VD, tq), F32)],
        compiler_params=_cparams(("parallel", "parallel", "arbitrary")),
        name="diff_attn",
    )(qt, k, vt, dlam, subln_g.reshape(DIFF_VD, 1))


def _mla_call(qt, k, vt, *, tq, tk):
    b, _, sq = qt.shape
    sk = k.shape[1]
    return pl.pallas_call(
        _mla_kernel,
        grid=(b, sq // tq, sk // tk),
        in_specs=_attn_specs(qt, k, vt, tq, tk),
        out_specs=pl.BlockSpec((1, MLA_HEADS * MLA_V, tq), lambda b_, i, j: (b_, 0, i)),
        out_shape=jax.ShapeDtypeStruct((b, MLA_HEADS * MLA_V, sq), BF16),
        scratch_shapes=[pltpu.VMEM((MLA_HEADS, tq), F32), pltpu.VMEM((MLA_HEADS, tq), F32),
                        pltpu.VMEM((MLA_HEADS, MLA_V, tq), F32)],
        compiler_params=_cparams(("parallel", "parallel", "arbitrary")),
        name="mla_attn",
    )(qt, k, vt)


HALO = 8


def _pool_kernel(xp_ref, x_ref, xn_ref, mod_ref, g_ref, o_ref, hc_ref, *, tm, seq):
    i = pl.program_id(1)
    g = g_ref[...]
    shift, scale = mod_ref[0, 3:4, :], mod_ref[0, 4:5, :]
    hc_ref[0:HALO, :] = jnp.where(i > 0, _modulate(xp_ref[0], g, shift, scale), 0.0)
    hc_ref[HALO:HALO + tm, :] = _modulate(x_ref[0], g, shift, scale)
    hc_ref[HALO + tm:, :] = jnp.where(i < pl.num_programs(1) - 1, _modulate(xn_ref[0], g, shift, scale), 0.0)
    gc = x_ref.shape[2] // len(POOL_WINDOWS)
    t = i * tm + lax.broadcasted_iota(jnp.int32, (tm, 1), 0)
    for k, w in enumerate(POOL_WINDOWS):
        cols = slice(k * gc, (k + 1) * gc)
        acc = hc_ref[HALO - w // 2:HALO - w // 2 + tm, cols]
        for s in range(1, w):
            acc = acc + hc_ref[HALO - w // 2 + s:HALO - w // 2 + s + tm, cols]
        cnt = (jnp.minimum(t + w // 2, seq) - jnp.maximum(t - w // 2, 0)).astype(F32)
        o_ref[0, :, cols] = (acc / cnt - hc_ref[HALO:HALO + tm, cols]).astype(BF16)


def _pool_call(x, mod, g, *, mod_base, mod_per_batch, tm):
    b, seq, d = x.shape
    nb = tm // HALO
    last = seq // HALO - 1
    mod_spec = pl.BlockSpec((1, N_MOD, d), lambda b_, i: (mod_base + b_ * mod_per_batch, 0, 0))
    return pl.pallas_call(
        functools.partial(_pool_kernel, tm=tm, seq=seq),
        grid=(b, seq // tm),
        in_specs=[
            pl.BlockSpec((1, HALO, d), lambda b_, i: (b_, jnp.maximum(i * nb - 1, 0), 0)),
            pl.BlockSpec((1, tm, d), lambda b_, i: (b_, i, 0)),
            pl.BlockSpec((1, HALO, d), lambda b_, i: (b_, jnp.minimum((i + 1) * nb, last), 0)),
            mod_spec,
            pl.BlockSpec((1, d), lambda b_, i: (0, 0)),
        ],
        out_specs=pl.BlockSpec((1, tm, d), lambda b_, i: (b_, i, 0)),
        out_shape=jax.ShapeDtypeStruct((b, seq, d), BF16),
        scratch_shapes=[pltpu.VMEM((tm + 2 * HALO, d), F32)],
        compiler_params=_cparams(("parallel", "parallel")),
        name="pool_mixer",
    )(x, x, x, mod, g.reshape(1, d))


def _rope_tables(seq, dim, pad_lo, pad_hi, reps):
    q = dim // 4
    t = jnp.arange(seq)
    freqs = ROPE_BASE ** (-jnp.arange(q, dtype=F32) / q)
    ar = (t // GRID_W).astype(F32)[:, None] * freqs
    ac = (t % GRID_W).astype(F32)[:, None] * freqs
    cos = jnp.concatenate([jnp.cos(ar), jnp.cos(ar), jnp.cos(ac), jnp.cos(ac)], axis=-1)
    sin = jnp.concatenate([-jnp.sin(ar), jnp.sin(ar), -jnp.sin(ac), jnp.sin(ac)], axis=-1)
    one = lambda w: jnp.ones((seq, w), F32)
    zero = lambda w: jnp.zeros((seq, w), F32)
    cos = jnp.tile(jnp.concatenate([one(pad_lo), cos, one(pad_hi)], axis=-1), (1, reps))
    sin = jnp.tile(jnp.concatenate([zero(pad_lo), sin, zero(pad_hi)], axis=-1), (1, reps))
    return cos, sin


def _group_matrix(sizes):
    m = jnp.zeros((MXU_DIM, MXU_DIM), F32)
    lo = 0
    while lo < MXU_DIM:
        for size, width in sizes:
            if size:
                m = m.at[lo:lo + width, lo:lo + width].set(1.0 / size)
            lo += width
    return m.astype(BF16)


def _attn_params(w_in, qk_g, q_a_g, w_q_b, kv_a_g, w_kv_b, nope_g, rope_g):
    d = w_in.shape[0]
    hq = MLA_NOPE + MLA_ROPE
    pad = LANES - hq
    w_in_p = jnp.concatenate([w_in[:, :IN_KR], jnp.zeros((d, MLA_NOPE), F32), w_in[:, IN_KR:],
                              jnp.zeros((d, pad), F32)], axis=-1)
    wq = w_q_b.reshape(MLA_Q_RANK, MLA_HEADS, hq)
    wq = jnp.concatenate([wq, jnp.zeros((MLA_Q_RANK, MLA_HEADS, pad), F32)], axis=-1)
    wkv = w_kv_b.reshape(MLA_KV_RANK, MLA_HEADS, MLA_NOPE + MLA_V)
    wk = jnp.concatenate([wkv[..., :MLA_NOPE], jnp.zeros((MLA_KV_RANK, MLA_HEADS, LANES - MLA_NOPE), F32)], axis=-1)
    wv = wkv[..., MLA_NOPE:]
    q_scale = hq ** -0.5
    zeros = lambda w: jnp.zeros((w,), F32)
    gqm = jnp.tile(jnp.concatenate([nope_g[0], rope_g[0], zeros(pad)]), MLA_HEADS) * q_scale
    gkm = jnp.tile(jnp.concatenate([nope_g[1], zeros(LANES - MLA_NOPE)]), MLA_HEADS)
    gkr = jnp.concatenate([zeros(MLA_NOPE), rope_g[1], zeros(pad)])
    row = lambda v: v.reshape(1, -1)
    return {
        "w_in": w_in_p.astype(BF16),
        "w_q_b": wq.reshape(MLA_Q_RANK, MLA_HEADS * LANES).astype(BF16),
        "w_kv_b": jnp.concatenate([wk.reshape(MLA_KV_RANK, -1), wv.reshape(MLA_KV_RANK, -1)], axis=-1).astype(BF16),
        "g64": _group_matrix([(DIFF_HD, DIFF_HD)]),
        "gm": _group_matrix([(MLA_NOPE, MLA_NOPE), (MLA_ROPE, MLA_ROPE), (0, pad)]),
        "gqd": row(jnp.tile(qk_g[0], 2 * DIFF_HEADS) * DIFF_HD ** -0.5),
        "gkd": row(jnp.tile(qk_g[1], 2 * DIFF_HEADS)),
        "gqa": row(q_a_g), "gkva": row(kv_a_g),
        "gqm": row(gqm), "gkm": row(gkm), "gkr": row(gkr),
    }


def _diff_query_t(qd):
    b, s, _ = qd.shape
    q = qd.reshape(b, s, DIFF_HEADS, 2, DIFF_HD)
    z = jnp.zeros_like(q[:, :, :, 0])
    q0 = jnp.concatenate([q[:, :, :, 0], z], axis=-1)
    q1 = jnp.concatenate([z, q[:, :, :, 1]], axis=-1)
    q = jnp.stack([q0, q1], axis=3).reshape(b, s, 2 * DIFF_HEADS * LANES)
    return jnp.swapaxes(q, 1, 2)


def _attention(proj_l, proj_c, dlam, subln_g, lam_init, with_ctx_queries, tq, tk):
    qd_l, kd_l, vd_l, qm_l, km_l, vm_l = proj_l
    qd_c, kd_c, vd_c, qm_c, km_c, vm_c = proj_c
    t = lambda a: jnp.swapaxes(a, 1, 2)
    cat = lambda c, l: jnp.concatenate([c, l], axis=1)

    def run(qd, qm, kd, vd, km, vm, tq_, tk_):
        od = _diff_call(_diff_query_t(qd), kd, t(vd), dlam, subln_g, lam_init, tq=tq_, tk=tk_)
        om = _mla_call(t(qm), km, t(vm), tq=tq_, tk=tk_)
        return jnp.concatenate([t(od), t(om)], axis=-1)

    merged_l = run(qd_l, qm_l, cat(kd_c, kd_l), cat(vd_c, vd_l), cat(km_c, km_l), cat(vm_c, vm_l), tq, tk)
    merged_c = None
    if with_ctx_queries:
        lc = qd_c.shape[1]
        merged_c = run(qd_c, qm_c, kd_c, vd_c, km_c, vm_c, lc, lc)
    return merged_l, merged_c


def _largest_tile(n, cap, mult):
    best = None
    for t in range(mult, min(n, cap) + 1, mult):
        if n % t == 0:
            best = t
    assert best is not None, (n, cap, mult)
    return best


def kernel(x, c, ctx, c_ctx, mod_w, mod_b, norm_g, ffn_w_gate, ffn_w_up, ffn_w_down, attn_w_in, diff_qk_g,
           diff_lambda, diff_subln_g, mla_q_a_g, mla_w_q_b, mla_kv_a_g, mla_w_kv_b, mla_nope_g, mla_rope_g,
           attn_w_out, pool_w, pool_scale):
    b, s, d = x.shape
    lc = ctx.shape[1]
    depth = mod_w.shape[0]
    assert s % GRID_W == 0 and d == 1024

    tm_l = _largest_tile(s, 512, 128)
    tm_c = _largest_tile(lc, 512, 8)
    tq = _largest_tile(s, 512, 128)
    tk = _largest_tile(s + lc, 768, 128)

    rows = -(-(b + 1) // 8) * 8
    cvec = jnp.concatenate([c, c_ctx[None], jnp.zeros((rows - b - 1, d), F32)], axis=0)
    mods = _mod_call(cvec, mod_w, mod_b).reshape(depth, rows, N_MOD, d)

    wg, wu, wd = ffn_w_gate.astype(BF16), ffn_w_up.astype(BF16), ffn_w_down.astype(BF16)
    cos_d, sin_d = _rope_tables(s, DIFF_HD, 0, 0, LANES // DIFF_HD)
    cos_m, sin_m = _rope_tables(s, MLA_ROPE, MLA_NOPE, LANES - MLA_NOPE - MLA_ROPE, 1)
    tab_l = (cos_d, sin_d, cos_m, sin_m)
    ones, zeros = jnp.ones((lc, LANES), F32), jnp.zeros((lc, LANES), F32)
    tab_c = (ones, zeros, ones, zeros)

    xl = x.reshape(b * s, d)
    xc = ctx.reshape(b * lc, d)
    lat = dict(tiles_per_mod=s // tm_l, mod_base=0, tm=tm_l)
    cx = dict(tiles_per_mod=b * lc // tm_c + 1, mod_base=b, tm=tm_c)

    for layer in range(depth):
        even = layer % 2 == 0
        ctx_out = layer < depth - 1
        ctx_in = ctx_out or even
        i = layer // 2
        g = norm_g[layer]
        mod = mods[layer]
        fw1 = (wg[layer, 0], wu[layer, 0], wd[layer, 0])
        fw2 = (wg[layer, 1], wu[layer, 1], wd[layer, 1])

        xl = _ffn_call(xl, mod, g[0], *fw1, idx=0, **lat)
        if ctx_in:
            xc = _ffn_call(xc, mod, g[0], *fw1, idx=0, **cx)

        if even:
            lam_init = 0.8 - 0.6 * math.exp(-0.3 * layer)
            pw = _attn_params(attn_w_in[i], diff_qk_g[i], mla_q_a_g[i], mla_w_q_b[i], mla_kv_a_g[i],
                              mla_w_kv_b[i], mla_nope_g[i], mla_rope_g[i])
            pl_ = _proj_call(xl, mod, g[1], pw, tab_l, tiles_per_seq=s // tm_l, **lat)
            pc_ = _proj_call(xc, mod, g[1], pw, tab_c, tiles_per_seq=lc // tm_c, **cx)
            pl_ = [a.reshape(b, s, -1) for a in pl_]
            pc_ = [a.reshape(b, lc, -1) for a in pc_]
            ml, mc = _attention(pl_, pc_, diff_lambda[i].astype(F32), diff_subln_g[i], lam_init, ctx_out, tq, tk)
            wo = attn_w_out[i].astype(BF16)
            mix_l = ("attn", ml.reshape(b * s, -1), wo)
            mix_c = ("attn", mc.reshape(b * lc, -1), wo) if ctx_out else None
        else:
            wp = pool_w[i].astype(BF16)
            dl_ = _pool_call(xl.reshape(b, s, d), mod, g[1], mod_base=0, mod_per_batch=1, tm=tm_l)
            mix_l = ("pool", dl_.reshape(b * s, d), wp, pool_scale[i])
            mix_c = None
            if ctx_out:
                dc_ = _pool_call(xc.reshape(b, lc, d), mod, g[1], mod_base=b, mod_per_batch=0, tm=tm_c)
                mix_c = ("pool", dc_.reshape(b * lc, d), wp, pool_scale[i])

        xl = _ffn_call(xl, mod, g[2], *fw2, idx=2, mixer=mix_l, **lat)
        if ctx_out:
            xc = _ffn_call(xc, mod, g[2], *fw2, idx=2, mixer=mix_c, **cx)

    return xl.reshape(b, s, d)
```

```python
import functools
import math

import jax
import jax.numpy as jnp
from jax import lax
from jax.experimental import pallas as pl
from jax.experimental.pallas import tpu as pltpu

F32 = jnp.float32
BF16 = jnp.bfloat16

EPS = 1e-6
ROPE_BASE = 10000.0
GRID_W = 64
N_MOD = 9
DIFF_HEADS = 4
DIFF_HD = 64
DIFF_VD = 128
MLA_HEADS = 8
MLA_NOPE = 64
MLA_ROPE = 32
MLA_V = 64
MLA_Q_RANK = 384
MLA_KV_RANK = 256
POOL_WINDOWS = (2, 4, 8, 16)
LANES = 128
MXU_DIM = 256
VMEM_LIMIT = 56 * 1024 * 1024
LOG2E = math.log2(math.e)
MAX_SAFE_LOGIT = 60.0

IN_DQ, IN_DK, IN_DV, IN_CQ, IN_CKV, IN_KR = 0, 512, 1024, 1536, 1920, 2176
IN_W_PAD = 2304


def _cparams(sem):
    return pltpu.CompilerParams(dimension_semantics=sem, vmem_limit_bytes=VMEM_LIMIT)


def _resident(shape):
    nd = len(shape)
    return pl.BlockSpec(shape, lambda *_: (0,) * nd, pipeline_mode=pl.Buffered(1))


def _modulate(x, g, shift, scale):
    r = lax.rsqrt(jnp.mean(x * x, axis=-1, keepdims=True) + EPS)
    return (x * r) * g * (1.0 + scale) + shift


def _mod_kernel(c_ref, w_ref, b_ref, o_ref):
    c = c_ref[...]
    s = (c * (1.0 / (1.0 + jnp.exp(-c)))).astype(BF16)
    o_ref[0] = jnp.dot(s, w_ref[0].astype(BF16), preferred_element_type=F32) + b_ref[0]


def _mod_call(cvec, mod_w, mod_b):
    depth, d, n = mod_w.shape
    rows = cvec.shape[0]
    tn = 1024
    return pl.pallas_call(
        _mod_kernel,
        grid=(depth, n // tn),
        in_specs=[
            pl.BlockSpec((rows, d), lambda l, j: (0, 0)),
            pl.BlockSpec((1, d, tn), lambda l, j: (l, 0, j)),
            pl.BlockSpec((1, 1, tn), lambda l, j: (l, 0, j)),
        ],
        out_specs=pl.BlockSpec((1, rows, tn), lambda l, j: (l, 0, j)),
        out_shape=jax.ShapeDtypeStruct((depth, rows, n), F32),
        compiler_params=_cparams(("parallel", "parallel")),
        name="mod_vectors",
    )(cvec, mod_w, mod_b.reshape(depth, 1, n))


def _ffn_body(x, mod_ref, g_ref, wg_ref, wu_ref, wd_ref, o_ref, idx, fc):
    shift = mod_ref[0, 3 * idx:3 * idx + 1, :]
    scale = mod_ref[0, 3 * idx + 1:3 * idx + 2, :]
    gate = mod_ref[0, 3 * idx + 2:3 * idx + 3, :]
    h = _modulate(x, g_ref[...], shift, scale).astype(BF16)
    d_ff = wg_ref.shape[1]
    acc = jnp.zeros(x.shape, F32)
    for c in range(d_ff // fc):
        sl = slice(c * fc, (c + 1) * fc)
        gt = jnp.dot(h, wg_ref[:, sl], preferred_element_type=F32)
        up = jnp.dot(h, wu_ref[:, sl], preferred_element_type=F32)
        a = (gt * (1.0 / (1.0 + jnp.exp(-gt))) * up).astype(BF16)
        acc = acc + jnp.dot(a, wd_ref[sl, :], preferred_element_type=F32)
    o_ref[...] = x + (0.5 * gate) * acc


def _ffn_kernel(x_ref, mod_ref, g_ref, wg_ref, wu_ref, wd_ref, o_ref, *, idx, fc):
    _ffn_body(x_ref[...], mod_ref, g_ref, wg_ref, wu_ref, wd_ref, o_ref, idx, fc)


def _ffn_attn_kernel(x_ref, u_ref, wo_ref, mod_ref, g_ref, wg_ref, wu_ref, wd_ref, o_ref, *, idx, fc):
    y = jnp.dot(u_ref[...], wo_ref[...], preferred_element_type=F32)
    x = x_ref[...] + mod_ref[0, 5:6, :] * y
    _ffn_body(x, mod_ref, g_ref, wg_ref, wu_ref, wd_ref, o_ref, idx, fc)


def _ffn_pool_kernel(x_ref, u_ref, wp_ref, ps_ref, mod_ref, g_ref, wg_ref, wu_ref, wd_ref, o_ref, *, idx, fc):
    gc = wp_ref.shape[1]
    ys = [jnp.dot(u_ref[:, k * gc:(k + 1) * gc], wp_ref[k], preferred_element_type=F32)
          for k in range(wp_ref.shape[0])]
    y = jnp.concatenate(ys, axis=-1) * ps_ref[...]
    x = x_ref[...] + mod_ref[0, 5:6, :] * y
    _ffn_body(x, mod_ref, g_ref, wg_ref, wu_ref, wd_ref, o_ref, idx, fc)


def _ffn_call(x, mod, g, wg, wu, wd, *, idx, tiles_per_mod, mod_base, tm, mixer=None):
    n, d = x.shape
    d_ff = wg.shape[1]
    fc = MXU_DIM
    tok = pl.BlockSpec((tm, d), lambda i: (i, 0))
    mod_spec = pl.BlockSpec((1, N_MOD, d), lambda i: (mod_base + i // tiles_per_mod, 0, 0))
    common_specs = [mod_spec, _resident((1, d)), _resident((d, d_ff)), _resident((d, d_ff)),
                    _resident((d_ff, d))]
    common_args = [mod, g.reshape(1, d), wg, wu, wd]
    if mixer is None:
        kern = functools.partial(_ffn_kernel, idx=idx, fc=fc)
        specs, args = [tok] + common_specs, [x] + common_args
    elif mixer[0] == "attn":
        _, u, wo = mixer
        kern = functools.partial(_ffn_attn_kernel, idx=idx, fc=fc)
        specs = [tok, pl.BlockSpec((tm, u.shape[1]), lambda i: (i, 0)), _resident(wo.shape)] + common_specs
        args = [x, u, wo] + common_args
    else:
        _, u, wp, ps = mixer
        kern = functools.partial(_ffn_pool_kernel, idx=idx, fc=fc)
        specs = [tok, pl.BlockSpec((tm, u.shape[1]), lambda i: (i, 0)), _resident(wp.shape),
                 _resident((1, d))] + common_specs
        args = [x, u, wp, ps.reshape(1, d)] + common_args
    return pl.pallas_call(
        kern,
        grid=(n // tm,),
        in_specs=specs,
        out_specs=tok,
        out_shape=jax.ShapeDtypeStruct((n, d), F32),
        compiler_params=_cparams(("parallel",)),
        name="ffn_half",
    )(*args)


def _group_rms(x, gm_ref, gain):
    outs = []
    for j in range(x.shape[1] // MXU_DIM):
        xb = x[:, j * MXU_DIM:(j + 1) * MXU_DIM]
        ms = jnp.dot((xb * xb).astype(BF16), gm_ref[...], preferred_element_type=F32)
        outs.append(xb * lax.rsqrt(ms + EPS) * gain[:, j * MXU_DIM:(j + 1) * MXU_DIM])
    return jnp.concatenate(outs, axis=-1)


def _rope(x, cos, sin, half):
    outs = []
    lane = lax.broadcasted_iota(jnp.int32, (x.shape[0], LANES), 1)
    first = (lane % (2 * half)) < half
    for j in range(x.shape[1] // LANES):
        xb = x[:, j * LANES:(j + 1) * LANES]
        partner = jnp.where(first, pltpu.roll(xb, LANES - half, 1), pltpu.roll(xb, half, 1))
        outs.append(xb * cos + partner * sin)
    return jnp.concatenate(outs, axis=-1)


def _proj_kernel(x_ref, mod_ref, g_ref, win_ref, wqb_ref, wkvb_ref, g64_ref, gm_ref,
                 gqd_ref, gkd_ref, gqa_ref, gkva_ref, gqm_ref, gkm_ref, gkr_ref,
                 cd_ref, sd_ref, cm_ref, sm_ref,
                 qd_ref, kd_ref, vd_ref, qm_ref, km_ref, vm_ref):
    x = x_ref[...]
    h = _modulate(x, g_ref[...], mod_ref[0, 3:4, :], mod_ref[0, 4:5, :]).astype(BF16)

    def inp(lo, hi):
        return jnp.dot(h, win_ref[:, lo:hi], preferred_element_type=F32)

    cd, sd, cm, sm = cd_ref[...], sd_ref[...], cm_ref[...], sm_ref[...]

    dq = _rope(_group_rms(inp(IN_DQ, IN_DK), g64_ref, gqd_ref[...]), cd, sd, DIFF_HD // 4)
    qd_ref[...] = dq.astype(BF16)
    dk = _rope(_group_rms(inp(IN_DK, IN_DV), g64_ref, gkd_ref[...]), cd, sd, DIFF_HD // 4)
    kd_ref[...] = dk.astype(BF16)
    vd_ref[...] = inp(IN_DV, IN_CQ).astype(BF16)

    cq = inp(IN_CQ, IN_CKV)
    cq = cq * lax.rsqrt(jnp.mean(cq * cq, axis=-1, keepdims=True) + EPS) * gqa_ref[...]
    mq = jnp.dot(cq.astype(BF16), wqb_ref[...], preferred_element_type=F32)
    qm_ref[...] = _rope(_group_rms(mq, gm_ref, gqm_ref[...]), cm, sm, MLA_ROPE // 4).astype(BF16)

    ckv = inp(IN_CKV, IN_KR)
    ckv = ckv * lax.rsqrt(jnp.mean(ckv * ckv, axis=-1, keepdims=True) + EPS) * gkva_ref[...]
    kv = jnp.dot(ckv.astype(BF16), wkvb_ref[...], preferred_element_type=F32)
    n_k = MLA_HEADS * LANES
    kn = _group_rms(kv[:, :n_k], gm_ref, gkm_ref[...])
    vm_ref[...] = kv[:, n_k:].astype(BF16)

    kr_raw = inp(IN_KR, IN_W_PAD)
    kr2 = jnp.concatenate([kr_raw, kr_raw], axis=-1)
    kr = _group_rms(kr2, gm_ref, jnp.concatenate([gkr_ref[...], gkr_ref[...]], axis=-1))[:, :LANES]
    kr = _rope(kr, cm, sm, MLA_ROPE // 4)
    km_ref[...] = (kn + jnp.concatenate([kr] * MLA_HEADS, axis=-1)).astype(BF16)


def _proj_call(x, mod, g, pw, tables, *, tiles_per_mod, mod_base, tiles_per_seq, tm):
    n, d = x.shape
    tok = lambda w: pl.BlockSpec((tm, w), lambda i: (i, 0))
    tab = pl.BlockSpec((tm, LANES), lambda i: (i % tiles_per_seq, 0))
    mod_spec = pl.BlockSpec((1, N_MOD, d), lambda i: (mod_base + i // tiles_per_mod, 0, 0))
    consts = [pw["w_in"], pw["w_q_b"], pw["w_kv_b"], pw["g64"], pw["gm"],
              pw["gqd"], pw["gkd"], pw["gqa"], pw["gkva"], pw["gqm"], pw["gkm"], pw["gkr"]]
    widths = (512, 512, 512, 1024, 1024, 512)
    return pl.pallas_call(
        _proj_kernel,
        grid=(n // tm,),
        in_specs=[tok(d), mod_spec, _resident((1, d))] + [_resident(c.shape) for c in consts] + [tab] * 4,
        out_specs=[tok(w) for w in widths],
        out_shape=[jax.ShapeDtypeStruct((n, w), BF16) for w in widths],
        compiler_params=_cparams(("parallel",)),
        name="attn_proj",
    )(x, mod, g.reshape(1, d), *consts, *tables)


def _online_step(s, vt, m_ref, l_ref, acc_ref, k):
    m_old = m_ref[k:k + 1, :]
    m_new = jnp.maximum(m_old, jnp.max(s, axis=0, keepdims=True))
    alpha = jnp.exp2(m_old - m_new)
    p = jnp.exp2(s - m_new)
    l_ref[k:k + 1, :] = alpha * l_ref[k:k + 1, :] + jnp.sum(p, axis=0, keepdims=True)
    acc_ref[k] = alpha * acc_ref[k] + jnp.dot(vt, p.astype(BF16), preferred_element_type=F32)
    m_ref[k:k + 1, :] = m_new


def _init_stats(m_ref, l_ref, acc_ref):
    m_ref[...] = jnp.full(m_ref.shape, -jnp.inf, F32)
    l_ref[...] = jnp.zeros(l_ref.shape, F32)
    acc_ref[...] = jnp.zeros(acc_ref.shape, F32)


def _diff_finalize(lam_ref, sg_ref, o_ref, l_ref, acc_ref, lam_init):
    dl = lam_ref[...]
    lam = (jnp.exp(jnp.sum(dl[0:1] * dl[1:2], keepdims=True))
           - jnp.exp(jnp.sum(dl[2:3] * dl[3:4], keepdims=True)) + lam_init)
    for hd in range(DIFF_HEADS):
        o1 = acc_ref[2 * hd] / l_ref[2 * hd:2 * hd + 1, :]
        o2 = acc_ref[2 * hd + 1] / l_ref[2 * hd + 1:2 * hd + 2, :]
        o = o1 - lam * o2
        r = lax.rsqrt(jnp.mean(o * o, axis=0, keepdims=True) + EPS)
        o = (o * r) * sg_ref[...] * (1.0 - lam_init)
        o_ref[0, hd * DIFF_VD:(hd + 1) * DIFF_VD, :] = o.astype(BF16)


def _mla_finalize(o_ref, l_ref, acc_ref):
    for hd in range(MLA_HEADS):
        o = acc_ref[hd] / l_ref[hd:hd + 1, :]
        o_ref[0, hd * MLA_V:(hd + 1) * MLA_V, :] = o.astype(BF16)


def _diff_kernel(qt_ref, k_ref, vt_ref, lam_ref, sg_ref, o_ref, m_ref, l_ref, acc_ref, *, lam_init):
    j = pl.program_id(2)

    @pl.when(j == 0)
    def _():
        _init_stats(m_ref, l_ref, acc_ref)

    for hd in range(DIFF_HEADS):
        kpair = k_ref[0, :, hd * LANES:(hd + 1) * LANES]
        vt = vt_ref[0, 0, hd * DIFF_VD:(hd + 1) * DIFF_VD, :]
        for sub in range(2):
            k = 2 * hd + sub
            s = jnp.dot(kpair, qt_ref[0, k * LANES:(k + 1) * LANES, :], preferred_element_type=F32)
            _online_step(s, vt, m_ref, l_ref, acc_ref, k)

    @pl.when(j == pl.num_programs(2) - 1)
    def _():
        _diff_finalize(lam_ref, sg_ref, o_ref, l_ref, acc_ref, lam_init)


def _mla_kernel(qt_ref, k_ref, vt_ref, o_ref, m_ref, l_ref, acc_ref):
    j = pl.program_id(2)

    @pl.when(j == 0)
    def _():
        _init_stats(m_ref, l_ref, acc_ref)

    for hd in range(MLA_HEADS):
        s = jnp.dot(k_ref[0, :, hd * LANES:(hd + 1) * LANES], qt_ref[0, hd * LANES:(hd + 1) * LANES, :],
                    preferred_element_type=F32)
        _online_step(s, vt_ref[0, 0, hd * MLA_V:(hd + 1) * MLA_V, :], m_ref, l_ref, acc_ref, hd)

    @pl.when(j == pl.num_programs(2) - 1)
    def _():
        _mla_finalize(o_ref, l_ref, acc_ref)


def _bounded_chunks(qt_ref, k_ref, vt_ref, l_ref, acc_ref, n_heads, heads_per_kv, dv):
    l_ref[...] = jnp.zeros(l_ref.shape, F32)
    acc_ref[...] = jnp.zeros(acc_ref.shape, F32)
    kc = vt_ref.shape[3]

    def body(it, carry):
        r0 = pl.multiple_of(it * kc, kc)
        for k in range(n_heads):
            kv = k // heads_per_kv
            kk = k_ref[0, pl.ds(r0, kc), kv * LANES:(kv + 1) * LANES]
            s = jnp.dot(kk, qt_ref[0, k * LANES:(k + 1) * LANES, :], preferred_element_type=F32)
            p = jnp.exp2(s)
            l_ref[k:k + 1, :] += jnp.sum(p, axis=0, keepdims=True)
            acc_ref[k] += jnp.dot(vt_ref[0, it, kv * dv:(kv + 1) * dv, :], p.astype(BF16),
                                  preferred_element_type=F32)
        return carry

    lax.fori_loop(0, vt_ref.shape[1], body, 0)


def _diff_bounded_kernel(qt_ref, k_ref, vt_ref, lam_ref, sg_ref, o_ref, l_ref, acc_ref, *, lam_init):
    _bounded_chunks(qt_ref, k_ref, vt_ref, l_ref, acc_ref, 2 * DIFF_HEADS, 2, DIFF_VD)
    _diff_finalize(lam_ref, sg_ref, o_ref, l_ref, acc_ref, lam_init)


def _mla_bounded_kernel(qt_ref, k_ref, vt_ref, o_ref, l_ref, acc_ref):
    _bounded_chunks(qt_ref, k_ref, vt_ref, l_ref, acc_ref, MLA_HEADS, 1, MLA_V)
    _mla_finalize(o_ref, l_ref, acc_ref)


def _attn_call(kind, bounded, qt, k, vt, extra, lam_init, *, tq):
    b, _, sq = qt.shape
    nkc, vrows, kc = vt.shape[1:]
    if kind == "diff":
        n_heads, dv = 2 * DIFF_HEADS, DIFF_VD
        extra_specs = [pl.BlockSpec(extra[0].shape, lambda *_: (0, 0)), pl.BlockSpec((DIFF_VD, 1), lambda *_: (0, 0))]
        body = functools.partial(_diff_bounded_kernel if bounded else _diff_kernel, lam_init=lam_init)
    else:
        n_heads, dv = MLA_HEADS, MLA_V
        extra_specs = []
        body = _mla_bounded_kernel if bounded else _mla_kernel
    stats = [pltpu.VMEM((n_heads, tq), F32)] * (1 if bounded else 2)
    scratch = stats + [pltpu.VMEM((n_heads, dv, tq), F32)]
    out_shape = jax.ShapeDtypeStruct((b, vrows, sq), BF16)
    if bounded:
        grid = (b, sq // tq)
        in_specs = [
            pl.BlockSpec((1, qt.shape[1], tq), lambda b_, i: (b_, 0, i)),
            pl.BlockSpec((1,) + k.shape[1:], lambda b_, i: (b_, 0, 0), pipeline_mode=pl.Buffered(1)),
            pl.BlockSpec((1,) + vt.shape[1:], lambda b_, i: (b_, 0, 0, 0), pipeline_mode=pl.Buffered(1)),
        ]
        out_spec = pl.BlockSpec((1, vrows, tq), lambda b_, i: (b_, 0, i))
        sem = ("parallel", "arbitrary")
    else:
        grid = (b, sq // tq, nkc)
        in_specs = [
            pl.BlockSpec((1, qt.shape[1], tq), lambda b_, i, j: (b_, 0, i)),
            pl.BlockSpec((1, kc, k.shape[2]), lambda b_, i, j: (b_, j, 0)),
            pl.BlockSpec((1, 1, vrows, kc), lambda b_, i, j: (b_, j, 0, 0)),
        ]
        out_spec = pl.BlockSpec((1, vrows, tq), lambda b_, i, j: (b_, 0, i))
        sem = ("parallel", "parallel", "arbitrary")
    return pl.pallas_call(
        body, grid=grid, in_specs=in_specs + extra_specs, out_specs=out_spec, out_shape=out_shape,
        scratch_shapes=scratch, compiler_params=_cparams(sem),
        name=kind + ("_attn_bounded" if bounded else "_attn"),
    )(qt, k, vt, *extra)


HALO = 8


def _pool_kernel(xp_ref, x_ref, xn_ref, mod_ref, g_ref, o_ref, hc_ref, *, tm, seq):
    i = pl.program_id(1)
    g = g_ref[...]
    shift, scale = mod_ref[0, 3:4, :], mod_ref[0, 4:5, :]
    hc_ref[0:HALO, :] = jnp.where(i > 0, _modulate(xp_ref[0], g, shift, scale), 0.0)
    hc_ref[HALO:HALO + tm, :] = _modulate(x_ref[0], g, shift, scale)
    hc_ref[HALO + tm:, :] = jnp.where(i < pl.num_programs(1) - 1, _modulate(xn_ref[0], g, shift, scale), 0.0)
    gc = x_ref.shape[2] // len(POOL_WINDOWS)
    t = i * tm + lax.broadcasted_iota(jnp.int32, (tm, 1), 0)
    for k, w in enumerate(POOL_WINDOWS):
        cols = slice(k * gc, (k + 1) * gc)
        acc = hc_ref[HALO - w // 2:HALO - w // 2 + tm, cols]
        for s in range(1, w):
            acc = acc + hc_ref[HALO - w // 2 + s:HALO - w // 2 + s + tm, cols]
        cnt = (jnp.minimum(t + w // 2, seq) - jnp.maximum(t - w // 2, 0)).astype(F32)
        o_ref[0, :, cols] = (acc / cnt - hc_ref[HALO:HALO + tm, cols]).astype(BF16)


def _pool_call(x, mod, g, *, mod_base, mod_per_batch, tm):
    b, seq, d = x.shape
    nb = tm // HALO
    last = seq // HALO - 1
    mod_spec = pl.BlockSpec((1, N_MOD, d), lambda b_, i: (mod_base + b_ * mod_per_batch, 0, 0))
    return pl.pallas_call(
        functools.partial(_pool_kernel, tm=tm, seq=seq),
        grid=(b, seq // tm),
        in_specs=[
            pl.BlockSpec((1, HALO, d), lambda b_, i: (b_, jnp.maximum(i * nb - 1, 0), 0)),
            pl.BlockSpec((1, tm, d), lambda b_, i: (b_, i, 0)),
            pl.BlockSpec((1, HALO, d), lambda b_, i: (b_, jnp.minimum((i + 1) * nb, last), 0)),
            mod_spec,
            pl.BlockSpec((1, d), lambda b_, i: (0, 0)),
        ],
        out_specs=pl.BlockSpec((1, tm, d), lambda b_, i: (b_, i, 0)),
        out_shape=jax.ShapeDtypeStruct((b, seq, d), BF16),
        scratch_shapes=[pltpu.VMEM((tm + 2 * HALO, d), F32)],
        compiler_params=_cparams(("parallel", "parallel")),
        name="pool_mixer",
    )(x, x, x, mod, g.reshape(1, d))


def _rope_tables(seq, dim, pad_lo, pad_hi, reps):
    q = dim // 4
    t = jnp.arange(seq)
    freqs = ROPE_BASE ** (-jnp.arange(q, dtype=F32) / q)
    ar = (t // GRID_W).astype(F32)[:, None] * freqs
    ac = (t % GRID_W).astype(F32)[:, None] * freqs
    cos = jnp.concatenate([jnp.cos(ar), jnp.cos(ar), jnp.cos(ac), jnp.cos(ac)], axis=-1)
    sin = jnp.concatenate([-jnp.sin(ar), jnp.sin(ar), -jnp.sin(ac), jnp.sin(ac)], axis=-1)
    one = lambda w: jnp.ones((seq, w), F32)
    zero = lambda w: jnp.zeros((seq, w), F32)
    cos = jnp.tile(jnp.concatenate([one(pad_lo), cos, one(pad_hi)], axis=-1), (1, reps))
    sin = jnp.tile(jnp.concatenate([zero(pad_lo), sin, zero(pad_hi)], axis=-1), (1, reps))
    return cos, sin


def _group_matrix(sizes):
    m = jnp.zeros((MXU_DIM, MXU_DIM), F32)
    lo = 0
    while lo < MXU_DIM:
        for size, width in sizes:
            if size:
                m = m.at[lo:lo + width, lo:lo + width].set(1.0 / size)
            lo += width
    return m.astype(BF16)


def _attn_params(w_in, qk_g, q_a_g, w_q_b, kv_a_g, w_kv_b, nope_g, rope_g):
    d = w_in.shape[0]
    hq = MLA_NOPE + MLA_ROPE
    pad = LANES - hq
    w_in_p = jnp.concatenate([w_in[:, :IN_KR], jnp.zeros((d, MLA_NOPE), F32), w_in[:, IN_KR:],
                              jnp.zeros((d, pad), F32)], axis=-1)
    wq = w_q_b.reshape(MLA_Q_RANK, MLA_HEADS, hq)
    wq = jnp.concatenate([wq, jnp.zeros((MLA_Q_RANK, MLA_HEADS, pad), F32)], axis=-1)
    wkv = w_kv_b.reshape(MLA_KV_RANK, MLA_HEADS, MLA_NOPE + MLA_V)
    wk = jnp.concatenate([wkv[..., :MLA_NOPE], jnp.zeros((MLA_KV_RANK, MLA_HEADS, LANES - MLA_NOPE), F32)], axis=-1)
    wv = wkv[..., MLA_NOPE:]
    q_scale = hq ** -0.5 * LOG2E
    zeros = lambda w: jnp.zeros((w,), F32)
    gqm = jnp.tile(jnp.concatenate([nope_g[0], rope_g[0], zeros(pad)]), MLA_HEADS) * q_scale
    gkm = jnp.tile(jnp.concatenate([nope_g[1], zeros(LANES - MLA_NOPE)]), MLA_HEADS)
    gkr = jnp.concatenate([zeros(MLA_NOPE), rope_g[1], zeros(pad)])
    row = lambda v: v.reshape(1, -1)
    return {
        "w_in": w_in_p.astype(BF16),
        "w_q_b": wq.reshape(MLA_Q_RANK, MLA_HEADS * LANES).astype(BF16),
        "w_kv_b": jnp.concatenate([wk.reshape(MLA_KV_RANK, -1), wv.reshape(MLA_KV_RANK, -1)], axis=-1).astype(BF16),
        "g64": _group_matrix([(DIFF_HD, DIFF_HD)]),
        "gm": _group_matrix([(MLA_NOPE, MLA_NOPE), (MLA_ROPE, MLA_ROPE), (0, pad)]),
        "gqd": row(jnp.tile(qk_g[0], 2 * DIFF_HEADS) * (DIFF_HD ** -0.5 * LOG2E)),
        "gkd": row(jnp.tile(qk_g[1], 2 * DIFF_HEADS)),
        "gqa": row(q_a_g), "gkva": row(kv_a_g),
        "gqm": row(gqm), "gkm": row(gkm), "gkr": row(gkr),
    }


def _diff_query_t(qd):
    b, s, _ = qd.shape
    q = qd.reshape(b, s, DIFF_HEADS, 2, DIFF_HD)
    z = jnp.zeros_like(q[:, :, :, 0])
    q0 = jnp.concatenate([q[:, :, :, 0], z], axis=-1)
    q1 = jnp.concatenate([z, q[:, :, :, 1]], axis=-1)
    q = jnp.stack([q0, q1], axis=3).reshape(b, s, 2 * DIFF_HEADS * LANES)
    return jnp.swapaxes(q, 1, 2)


def _logit_bounds(qk_g, nope_g, rope_g):
    amax = lambda v: jnp.max(jnp.abs(v))
    diff = LOG2E * DIFF_HD ** 0.5 * amax(qk_g[0]) * amax(qk_g[1])
    qn = jnp.sqrt(MLA_NOPE * amax(nope_g[0]) ** 2 + MLA_ROPE * amax(rope_g[0]) ** 2)
    kn = jnp.sqrt(MLA_NOPE * amax(nope_g[1]) ** 2 + MLA_ROPE * amax(rope_g[1]) ** 2)
    mla = LOG2E * (MLA_NOPE + MLA_ROPE) ** -0.5 * qn * kn
    return diff, mla


def _attention(proj_l, proj_c, dlam, subln_g, lam_init, bounds, with_ctx_queries, tq, kc):
    qd_l, kd_l, vd_l, qm_l, km_l, vm_l = proj_l
    qd_c, kd_c, vd_c, qm_c, km_c, vm_c = proj_c
    t = lambda a: jnp.swapaxes(a, 1, 2)
    cat = lambda c, l: jnp.concatenate([c, l], axis=1)

    def chunk_t(v, kc_):
        b_, sk, w = v.shape
        return jnp.swapaxes(v.reshape(b_, sk // kc_, kc_, w), 2, 3)

    def core(kind, bound, qt, k, vt, extra, tq_):
        return lax.cond(
            bound <= MAX_SAFE_LOGIT,
            lambda: _attn_call(kind, True, qt, k, vt, extra, lam_init, tq=tq_),
            lambda: _attn_call(kind, False, qt, k, vt, extra, lam_init, tq=tq_))

    def run(qd, qm, kd, vd, km, vm, tq_, kc_):
        od = core("diff", bounds[0], _diff_query_t(qd), kd, chunk_t(vd, kc_),
                  (dlam, subln_g.reshape(DIFF_VD, 1)), tq_)
        om = core("mla", bounds[1], t(qm), km, chunk_t(vm, kc_), (), tq_)
        return jnp.concatenate([t(od), t(om)], axis=-1)

    merged_l = run(qd_l, qm_l, cat(kd_c, kd_l), cat(vd_c, vd_l), cat(km_c, km_l), cat(vm_c, vm_l), tq, kc)
    merged_c = None
    if with_ctx_queries:
        lc = qd_c.shape[1]
        merged_c = run(qd_c, qm_c, kd_c, vd_c, km_c, vm_c, lc, lc)
    return merged_l, merged_c


def _largest_tile(n, cap, mult):
    best = None
    for t in range(mult, min(n, cap) + 1, mult):
        if n % t == 0:
            best = t
    assert best is not None, (n, cap, mult)
    return best


def kernel(x, c, ctx, c_ctx, mod_w, mod_b, norm_g, ffn_w_gate, ffn_w_up, ffn_w_down, attn_w_in, diff_qk_g,
           diff_lambda, diff_subln_g, mla_q_a_g, mla_w_q_b, mla_kv_a_g, mla_w_kv_b, mla_nope_g, mla_rope_g,
           attn_w_out, pool_w, pool_scale):
    b, s, d = x.shape
    lc = ctx.shape[1]
    depth = mod_w.shape[0]
    assert s % GRID_W == 0 and d == 1024

    tm_l = _largest_tile(s, 512, 128)
    tm_c = _largest_tile(lc, 512, 8)
    tq = _largest_tile(s, 512, 128)
    tk = _largest_tile(s + lc, 768, 128)

    rows = -(-(b + 1) // 8) * 8
    cvec = jnp.concatenate([c, c_ctx[None], jnp.zeros((rows - b - 1, d), F32)], axis=0)
    mods = _mod_call(cvec, mod_w, mod_b).reshape(depth, rows, N_MOD, d)

    wg, wu, wd = ffn_w_gate.astype(BF16), ffn_w_up.astype(BF16), ffn_w_down.astype(BF16)
    cos_d, sin_d = _rope_tables(s, DIFF_HD, 0, 0, LANES // DIFF_HD)
    cos_m, sin_m = _rope_tables(s, MLA_ROPE, MLA_NOPE, LANES - MLA_NOPE - MLA_ROPE, 1)
    tab_l = (cos_d, sin_d, cos_m, sin_m)
    ones, zeros = jnp.ones((lc, LANES), F32), jnp.zeros((lc, LANES), F32)
    tab_c = (ones, zeros, ones, zeros)

    xl = x.reshape(b * s, d)
    xc = ctx.reshape(b * lc, d)
    lat = dict(tiles_per_mod=s // tm_l, mod_base=0, tm=tm_l)
    cx = dict(tiles_per_mod=b * lc // tm_c + 1, mod_base=b, tm=tm_c)

    for layer in range(depth):
        even = layer % 2 == 0
        ctx_out = layer < depth - 1
        ctx_in = ctx_out or even
        i = layer // 2
        g = norm_g[layer]
        mod = mods[layer]
        fw1 = (wg[layer, 0], wu[layer, 0], wd[layer, 0])
        fw2 = (wg[layer, 1], wu[layer, 1], wd[layer, 1])

        xl = _ffn_call(xl, mod, g[0], *fw1, idx=0, **lat)
        if ctx_in:
            xc = _ffn_call(xc, mod, g[0], *fw1, idx=0, **cx)

        if even:
            lam_init = 0.8 - 0.6 * math.exp(-0.3 * layer)
            pw = _attn_params(attn_w_in[i], diff_qk_g[i], mla_q_a_g[i], mla_w_q_b[i], mla_kv_a_g[i],
                              mla_w_kv_b[i], mla_nope_g[i], mla_rope_g[i])
            pl_ = _proj_call(xl, mod, g[1], pw, tab_l, tiles_per_seq=s // tm_l, **lat)
            pc_ = _proj_call(xc, mod, g[1], pw, tab_c, tiles_per_seq=lc // tm_c, **cx)
            pl_ = [a.reshape(b, s, -1) for a in pl_]
            pc_ = [a.reshape(b, lc, -1) for a in pc_]
            bounds = _logit_bounds(diff_qk_g[i], mla_nope_g[i], mla_rope_g[i])
            ml, mc = _attention(pl_, pc_, diff_lambda[i].astype(F32), diff_subln_g[i], lam_init, bounds,
                                ctx_out, tq, tk)
            wo = attn_w_out[i].astype(BF16)
            mix_l = ("attn", ml.reshape(b * s, -1), wo)
            mix_c = ("attn", mc.reshape(b * lc, -1), wo) if ctx_out else None
        else:
            wp = pool_w[i].astype(BF16)
            dl_ = _pool_call(xl.reshape(b, s, d), mod, g[1], mod_base=0, mod_per_batch=1, tm=tm_l)
            mix_l = ("pool", dl_.reshape(b * s, d), wp, pool_scale[i])
            mix_c = None
            if ctx_out:
                dc_ = _pool_call(xc.reshape(b, lc, d), mod, g[1], mod_base=b, mod_per_batch=0, tm=tm_c)
                mix_c = ("pool", dc_.reshape(b * lc, d), wp, pool_scale[i])

        xl = _ffn_call(xl, mod, g[2], *fw2, idx=2, mixer=mix_l, **lat)
        if ctx_out:
            xc = _ffn_call(xc, mod, g[2], *fw2, idx=2, mixer=mix_c, **cx)

    return xl.reshape(b, s, d)
```

```python
import functools
import math

import jax
import jax.numpy as jnp
from jax import lax
from jax.experimental import pallas as pl
from jax.experimental.pallas import tpu as pltpu

F32 = jnp.float32
BF16 = jnp.bfloat16

EPS = 1e-6
ROPE_BASE = 10000.0
GRID_W = 64
N_MOD = 9
DIFF_HEADS = 4
DIFF_HD = 64
DIFF_VD = 128
MLA_HEADS = 8
MLA_NOPE = 64
MLA_ROPE = 32
MLA_V = 64
MLA_Q_RANK = 384
MLA_KV_RANK = 256
POOL_WINDOWS = (2, 4, 8, 16)
LANES = 128
MXU_DIM = 256
KEY_CHUNK = MXU_DIM
VMEM_LIMIT = 56 * 1024 * 1024
LOG2E = math.log2(math.e)
MAX_SAFE_LOGIT = 60.0

IN_DQ, IN_DK, IN_DV, IN_CQ, IN_CKV, IN_KR = 0, 512, 1024, 1536, 1920, 2176
IN_W_PAD = 2304


def _cparams(sem):
    return pltpu.CompilerParams(dimension_semantics=sem, vmem_limit_bytes=VMEM_LIMIT)


def _resident(shape):
    nd = len(shape)
    return pl.BlockSpec(shape, lambda *_: (0,) * nd, pipeline_mode=pl.Buffered(1))


def _modulate(x, g, shift, scale):
    r = lax.rsqrt(jnp.mean(x * x, axis=-1, keepdims=True) + EPS)
    return (x * r) * g * (1.0 + scale) + shift


def _mod_kernel(c_ref, w_ref, b_ref, o_ref):
    c = c_ref[...]
    s = (c * (1.0 / (1.0 + jnp.exp(-c)))).astype(BF16)
    o_ref[0] = jnp.dot(s, w_ref[0].astype(BF16), preferred_element_type=F32) + b_ref[0]


def _mod_call(cvec, mod_w, mod_b):
    depth, d, n = mod_w.shape
    rows = cvec.shape[0]
    tn = 1024
    return pl.pallas_call(
        _mod_kernel,
        grid=(depth, n // tn),
        in_specs=[
            pl.BlockSpec((rows, d), lambda l, j: (0, 0)),
            pl.BlockSpec((1, d, tn), lambda l, j: (l, 0, j)),
            pl.BlockSpec((1, 1, tn), lambda l, j: (l, 0, j)),
        ],
        out_specs=pl.BlockSpec((1, rows, tn), lambda l, j: (l, 0, j)),
        out_shape=jax.ShapeDtypeStruct((depth, rows, n), F32),
        compiler_params=_cparams(("parallel", "parallel")),
        name="mod_vectors",
    )(cvec, mod_w, mod_b.reshape(depth, 1, n))


def _ffn_body(x, mod_ref, g_ref, wg_ref, wu_ref, wd_ref, o_ref, idx, fc):
    shift = mod_ref[0, 3 * idx:3 * idx + 1, :]
    scale = mod_ref[0, 3 * idx + 1:3 * idx + 2, :]
    gate = mod_ref[0, 3 * idx + 2:3 * idx + 3, :]
    h = _modulate(x, g_ref[...], shift, scale).astype(BF16)
    d_ff = wg_ref.shape[1]
    acc = jnp.zeros(x.shape, F32)
    for c in range(d_ff // fc):
        sl = slice(c * fc, (c + 1) * fc)
        gt = jnp.dot(h, wg_ref[:, sl], preferred_element_type=F32)
        up = jnp.dot(h, wu_ref[:, sl], preferred_element_type=F32)
        a = (gt * (1.0 / (1.0 + jnp.exp(-gt))) * up).astype(BF16)
        acc = acc + jnp.dot(a, wd_ref[sl, :], preferred_element_type=F32)
    o_ref[...] = x + (0.5 * gate) * acc


def _ffn_kernel(x_ref, mod_ref, g_ref, wg_ref, wu_ref, wd_ref, o_ref, *, idx, fc):
    _ffn_body(x_ref[...], mod_ref, g_ref, wg_ref, wu_ref, wd_ref, o_ref, idx, fc)


def _ffn_attn_kernel(x_ref, ud_ref, um_ref, wo_ref, mod_ref, g_ref, wg_ref, wu_ref, wd_ref, o_ref, *, idx, fc):
    nd = ud_ref.shape[1]
    y = (jnp.dot(ud_ref[...], wo_ref[:nd, :], preferred_element_type=F32)
         + jnp.dot(um_ref[...], wo_ref[nd:, :], preferred_element_type=F32))
    x = x_ref[...] + mod_ref[0, 5:6, :] * y
    _ffn_body(x, mod_ref, g_ref, wg_ref, wu_ref, wd_ref, o_ref, idx, fc)


def _ffn_pool_kernel(x_ref, u_ref, wp_ref, ps_ref, mod_ref, g_ref, wg_ref, wu_ref, wd_ref, o_ref, *, idx, fc):
    gc = wp_ref.shape[1]
    ys = [jnp.dot(u_ref[:, k * gc:(k + 1) * gc], wp_ref[k], preferred_element_type=F32)
          for k in range(wp_ref.shape[0])]
    y = jnp.concatenate(ys, axis=-1) * ps_ref[...]
    x = x_ref[...] + mod_ref[0, 5:6, :] * y
    _ffn_body(x, mod_ref, g_ref, wg_ref, wu_ref, wd_ref, o_ref, idx, fc)


def _ffn_call(x, mod, g, wg, wu, wd, *, idx, tiles_per_mod, mod_base, tm, mixer=None):
    n, d = x.shape
    d_ff = wg.shape[1]
    fc = MXU_DIM
    tok = pl.BlockSpec((tm, d), lambda i: (i, 0))
    mod_spec = pl.BlockSpec((1, N_MOD, d), lambda i: (mod_base + i // tiles_per_mod, 0, 0))
    common_specs = [mod_spec, _resident((1, d)), _resident((d, d_ff)), _resident((d, d_ff)),
                    _resident((d_ff, d))]
    common_args = [mod, g.reshape(1, d), wg, wu, wd]
    if mixer is None:
        kern = functools.partial(_ffn_kernel, idx=idx, fc=fc)
        specs, args = [tok] + common_specs, [x] + common_args
    elif mixer[0] == "attn":
        _, ud, um, wo = mixer
        kern = functools.partial(_ffn_attn_kernel, idx=idx, fc=fc)
        specs = [tok, pl.BlockSpec((tm, ud.shape[1]), lambda i: (i, 0)),
                 pl.BlockSpec((tm, um.shape[1]), lambda i: (i, 0)), _resident(wo.shape)] + common_specs
        args = [x, ud, um, wo] + common_args
    else:
        _, u, wp, ps = mixer
        kern = functools.partial(_ffn_pool_kernel, idx=idx, fc=fc)
        specs = [tok, pl.BlockSpec((tm, u.shape[1]), lambda i: (i, 0)), _resident(wp.shape),
                 _resident((1, d))] + common_specs
        args = [x, u, wp, ps.reshape(1, d)] + common_args
    return pl.pallas_call(
        kern,
        grid=(n // tm,),
        in_specs=specs,
        out_specs=tok,
        out_shape=jax.ShapeDtypeStruct((n, d), F32),
        compiler_params=_cparams(("parallel",)),
        name="ffn_half",
    )(*args)


def _group_rms(x, gm_ref, gain):
    outs = []
    for j in range(x.shape[1] // MXU_DIM):
        xb = x[:, j * MXU_DIM:(j + 1) * MXU_DIM]
        ms = jnp.dot((xb * xb).astype(BF16), gm_ref[...], preferred_element_type=F32)
        outs.append(xb * lax.rsqrt(ms + EPS) * gain[:, j * MXU_DIM:(j + 1) * MXU_DIM])
    return jnp.concatenate(outs, axis=-1)


def _rope(x, cos, sin, half):
    outs = []
    lane = lax.broadcasted_iota(jnp.int32, (x.shape[0], LANES), 1)
    first = (lane % (2 * half)) < half
    for j in range(x.shape[1] // LANES):
        xb = x[:, j * LANES:(j + 1) * LANES]
        partner = jnp.where(first, pltpu.roll(xb, LANES - half, 1), pltpu.roll(xb, half, 1))
        outs.append(xb * cos + partner * sin)
    return jnp.concatenate(outs, axis=-1)


def _store_values_t(v, vt_ref):
    for j in range(v.shape[1] // LANES):
        t = v[:, j * LANES:(j + 1) * LANES].T.astype(BF16)
        for c in range(vt_ref.shape[1]):
            vt_ref[0, c, j * LANES:(j + 1) * LANES, :] = t[:, c * KEY_CHUNK:(c + 1) * KEY_CHUNK]


def _proj_kernel(x_ref, mod_ref, g_ref, win_ref, wqb_ref, wkvb_ref, g64_ref, gm_ref,
                 gqd_ref, gkd_ref, gqa_ref, gkva_ref, gqm_ref, gkm_ref, gkr_ref,
                 cd_ref, sd_ref, cm_ref, sm_ref, *refs):
    qdt_ref, qmt_ref, kd_ref, km_ref, vdt_ref, vmt_ref = refs[-6:]
    x = x_ref[...]
    h = _modulate(x, g_ref[...], mod_ref[0, 3:4, :], mod_ref[0, 4:5, :]).astype(BF16)

    def inp(lo, hi):
        return jnp.dot(h, win_ref[:, lo:hi], preferred_element_type=F32)

    cd, sd, cm, sm = cd_ref[...], sd_ref[...], cm_ref[...], sm_ref[...]

    dq = _rope(_group_rms(inp(IN_DQ, IN_DK), g64_ref, gqd_ref[...]), cd, sd, DIFF_HD // 4)
    row = lax.broadcasted_iota(jnp.int32, (LANES, x.shape[0]), 0)
    for hd in range(DIFF_HEADS):
        t = dq[:, hd * LANES:(hd + 1) * LANES].T
        qdt_ref[0, (2 * hd) * LANES:(2 * hd + 1) * LANES, :] = jnp.where(row < DIFF_HD, t, 0.0).astype(BF16)
        qdt_ref[0, (2 * hd + 1) * LANES:(2 * hd + 2) * LANES, :] = jnp.where(row < DIFF_HD, 0.0, t).astype(BF16)
    dk = _rope(_group_rms(inp(IN_DK, IN_DV), g64_ref, gkd_ref[...]), cd, sd, DIFF_HD // 4)
    kd_ref[0] = dk.astype(BF16)
    _store_values_t(inp(IN_DV, IN_CQ), vdt_ref)

    cq = inp(IN_CQ, IN_CKV)
    cq = cq * lax.rsqrt(jnp.mean(cq * cq, axis=-1, keepdims=True) + EPS) * gqa_ref[...]
    mq = jnp.dot(cq.astype(BF16), wqb_ref[...], preferred_element_type=F32)
    mq = _rope(_group_rms(mq, gm_ref, gqm_ref[...]), cm, sm, MLA_ROPE // 4)
    for hd in range(MLA_HEADS):
        qmt_ref[0, hd * LANES:(hd + 1) * LANES, :] = mq[:, hd * LANES:(hd + 1) * LANES].T.astype(BF16)

    ckv = inp(IN_CKV, IN_KR)
    ckv = ckv * lax.rsqrt(jnp.mean(ckv * ckv, axis=-1, keepdims=True) + EPS) * gkva_ref[...]
    kv = jnp.dot(ckv.astype(BF16), wkvb_ref[...], preferred_element_type=F32)
    n_k = MLA_HEADS * LANES
    kn = _group_rms(kv[:, :n_k], gm_ref, gkm_ref[...])
    _store_values_t(kv[:, n_k:], vmt_ref)

    kr_raw = inp(IN_KR, IN_W_PAD)
    kr2 = jnp.concatenate([kr_raw, kr_raw], axis=-1)
    kr = _group_rms(kr2, gm_ref, jnp.concatenate([gkr_ref[...], gkr_ref[...]], axis=-1))[:, :LANES]
    kr = _rope(kr, cm, sm, MLA_ROPE // 4)
    km_ref[0] = (kn + jnp.concatenate([kr] * MLA_HEADS, axis=-1)).astype(BF16)


def _proj_call(x, mod, g, pw, tables, kv_bufs, *, seq, key_len, key_off, tiles_per_mod, mod_base, tm):
    n, d = x.shape
    b = n // seq
    tps = seq // tm
    cpt = tm // KEY_CHUNK
    assert key_off % tm == 0 and tm % KEY_CHUNK == 0
    tab = pl.BlockSpec((tm, LANES), lambda i: (i % tps, 0))
    mod_spec = pl.BlockSpec((1, N_MOD, d), lambda i: (mod_base + i // tiles_per_mod, 0, 0))
    consts = [pw["w_in"], pw["w_q_b"], pw["w_kv_b"], pw["g64"], pw["gm"],
              pw["gqd"], pw["gkd"], pw["gqa"], pw["gkva"], pw["gqm"], pw["gkm"], pw["gkr"]]
    qrows = 2 * DIFF_HEADS * LANES
    qspec = pl.BlockSpec((1, qrows, tm), lambda i: (i // tps, 0, i % tps))
    kspec = lambda w: pl.BlockSpec((1, tm, w), lambda i: (i // tps, key_off // tm + i % tps, 0))
    vspec = pl.BlockSpec((1, cpt, 512, KEY_CHUNK), lambda i: (i // tps, key_off // tm + i % tps, 0, 0))
    nkc = key_len // KEY_CHUNK
    out_shape = [jax.ShapeDtypeStruct((b, qrows, seq), BF16), jax.ShapeDtypeStruct((b, qrows, seq), BF16),
                 jax.ShapeDtypeStruct((b, key_len, 512), BF16), jax.ShapeDtypeStruct((b, key_len, qrows), BF16),
                 jax.ShapeDtypeStruct((b, nkc, 512, KEY_CHUNK), BF16),
                 jax.ShapeDtypeStruct((b, nkc, 512, KEY_CHUNK), BF16)]
    in_specs = ([pl.BlockSpec((tm, d), lambda i: (i, 0)), mod_spec, _resident((1, d))]
                + [_resident(c.shape) for c in consts] + [tab] * 4)
    args = [x, mod, g.reshape(1, d), *consts, *tables]
    aliases = {}
    if kv_bufs is not None:
        aliases = {len(args) + j: 2 + j for j in range(4)}
        in_specs += [pl.BlockSpec(memory_space=pl.ANY)] * 4
        args += list(kv_bufs)
    return pl.pallas_call(
        _proj_kernel,
        grid=(n // tm,),
        in_specs=in_specs,
        out_specs=[qspec, qspec, kspec(512), kspec(qrows), vspec, vspec],
        out_shape=out_shape,
        input_output_aliases=aliases,
        compiler_params=_cparams(("parallel",)),
        name="attn_proj",
    )(*args)


def _online_step(s, vt, m_ref, l_ref, acc_ref, k):
    m_old = m_ref[k:k + 1, :]
    m_new = jnp.maximum(m_old, jnp.max(s, axis=0, keepdims=True))
    alpha = jnp.exp2(m_old - m_new)
    p = jnp.exp2(s - m_new)
    l_ref[k:k + 1, :] = alpha * l_ref[k:k + 1, :] + jnp.sum(p, axis=0, keepdims=True)
    acc_ref[k] = alpha * acc_ref[k] + jnp.dot(vt, p.astype(BF16), preferred_element_type=F32)
    m_ref[k:k + 1, :] = m_new


def _init_stats(m_ref, l_ref, acc_ref):
    m_ref[...] = jnp.full(m_ref.shape, -jnp.inf, F32)
    l_ref[...] = jnp.zeros(l_ref.shape, F32)
    acc_ref[...] = jnp.zeros(acc_ref.shape, F32)


def _diff_finalize(lam_ref, sg_ref, o_ref, l_ref, acc_ref, lam_init):
    dl = lam_ref[...]
    lam = (jnp.exp(jnp.sum(dl[0:1] * dl[1:2], keepdims=True))
           - jnp.exp(jnp.sum(dl[2:3] * dl[3:4], keepdims=True)) + lam_init)
    for hd in range(DIFF_HEADS):
        o1 = acc_ref[2 * hd] / l_ref[2 * hd:2 * hd + 1, :]
        o2 = acc_ref[2 * hd + 1] / l_ref[2 * hd + 1:2 * hd + 2, :]
        o = o1 - lam * o2
        r = lax.rsqrt(jnp.mean(o * o, axis=0, keepdims=True) + EPS)
        o = (o * r) * sg_ref[...] * (1.0 - lam_init)
        o_ref[0, :, hd * DIFF_VD:(hd + 1) * DIFF_VD] = o.T.astype(BF16)


def _mla_finalize(o_ref, l_ref, acc_ref):
    for pair in range(MLA_HEADS // 2):
        o = jnp.concatenate([acc_ref[2 * pair + j] / l_ref[2 * pair + j:2 * pair + j + 1, :] for j in range(2)],
                            axis=0)
        o_ref[0, :, pair * LANES:(pair + 1) * LANES] = o.T.astype(BF16)


def _online_kernel(qt_ref, k_ref, vt_ref, *refs, n_heads, heads_per_kv, dv, finalize):
    m_ref, l_ref, acc_ref = refs[-3:]
    j = pl.program_id(2)

    @pl.when(j == 0)
    def _():
        _init_stats(m_ref, l_ref, acc_ref)

    for k in range(n_heads):
        kv = k // heads_per_kv
        s = jnp.dot(k_ref[0, :, kv * LANES:(kv + 1) * LANES], qt_ref[0, k * LANES:(k + 1) * LANES, :],
                    preferred_element_type=F32)
        _online_step(s, vt_ref[0, 0, kv * dv:(kv + 1) * dv, :], m_ref, l_ref, acc_ref, k)

    @pl.when(j == pl.num_programs(2) - 1)
    def _():
        finalize(*refs[:-3], l_ref, acc_ref)


def _bounded_kernel(qt_ref, k_ref, vt_ref, *refs, n_heads, heads_per_kv, dv, cpi, finalize):
    l_ref, acc_ref = refs[-2:]
    l_ref[...] = jnp.zeros(l_ref.shape, F32)
    acc_ref[...] = jnp.zeros(acc_ref.shape, F32)
    rows = cpi * KEY_CHUNK

    def body(it, carry):
        r0 = pl.multiple_of(it * rows, KEY_CHUNK)
        for k in range(n_heads):
            kv = k // heads_per_kv
            s = jnp.dot(k_ref[0, pl.ds(r0, rows), kv * LANES:(kv + 1) * LANES],
                        qt_ref[0, k * LANES:(k + 1) * LANES, :], preferred_element_type=F32)
            p = jnp.exp2(s)
            l_ref[k:k + 1, :] += jnp.sum(p, axis=0, keepdims=True)
            pb = p.astype(BF16)
            pv = None
            for u in range(cpi):
                d = jnp.dot(vt_ref[0, it * cpi + u, kv * dv:(kv + 1) * dv, :],
                            pb[u * KEY_CHUNK:(u + 1) * KEY_CHUNK, :], preferred_element_type=F32)
                pv = d if pv is None else pv + d
            acc_ref[k] += pv
        return carry

    lax.fori_loop(0, vt_ref.shape[1] // cpi, body, 0)
    finalize(*refs[:-2], l_ref, acc_ref)


def _attn_call(kind, bounded, qt, k, vt, extra, lam_init, *, tq, key_block, n_keys, cpi):
    b, _, sq = qt.shape
    nkc = n_keys // KEY_CHUNK
    if kind == "diff":
        dims = dict(n_heads=2 * DIFF_HEADS, heads_per_kv=2, dv=DIFF_VD,
                    finalize=functools.partial(_diff_finalize, lam_init=lam_init))
        extra_specs = [pl.BlockSpec(extra[0].shape, lambda *_: (0, 0)), pl.BlockSpec((DIFF_VD, 1), lambda *_: (0, 0))]
    else:
        dims = dict(n_heads=MLA_HEADS, heads_per_kv=1, dv=MLA_V, finalize=_mla_finalize)
        extra_specs = []
    n_heads, dv = dims["n_heads"], dims["dv"]
    stats = [pltpu.VMEM((n_heads, tq), F32)] * (1 if bounded else 2)
    scratch = stats + [pltpu.VMEM((n_heads, dv, tq), F32)]
    vrows = vt.shape[2]
    out_shape = jax.ShapeDtypeStruct((b, sq, vrows), BF16)
    if bounded:
        body = functools.partial(_bounded_kernel, cpi=cpi, **dims)
        grid = (b, sq // tq)
        in_specs = [
            pl.BlockSpec((1, qt.shape[1], tq), lambda b_, i: (b_, 0, i)),
            pl.BlockSpec((1, n_keys, k.shape[2]), lambda b_, i: (b_, key_block, 0), pipeline_mode=pl.Buffered(1)),
            pl.BlockSpec((1, nkc, vrows, KEY_CHUNK), lambda b_, i: (b_, key_block, 0, 0),
                         pipeline_mode=pl.Buffered(1)),
        ]
        out_spec = pl.BlockSpec((1, tq, vrows), lambda b_, i: (b_, i, 0))
        sem = ("parallel", "arbitrary")
    else:
        body = functools.partial(_online_kernel, **dims)
        grid = (b, sq // tq, nkc)
        in_specs = [
            pl.BlockSpec((1, qt.shape[1], tq), lambda b_, i, j: (b_, 0, i)),
            pl.BlockSpec((1, KEY_CHUNK, k.shape[2]), lambda b_, i, j: (b_, key_block * nkc + j, 0)),
            pl.BlockSpec((1, 1, vrows, KEY_CHUNK), lambda b_, i, j: (b_, key_block * nkc + j, 0, 0)),
        ]
        out_spec = pl.BlockSpec((1, tq, vrows), lambda b_, i, j: (b_, i, 0))
        sem = ("parallel", "parallel", "arbitrary")
    return pl.pallas_call(
        body, grid=grid, in_specs=in_specs + extra_specs, out_specs=out_spec, out_shape=out_shape,
        scratch_shapes=scratch, compiler_params=_cparams(sem),
        name=kind + ("_attn_bounded" if bounded else "_attn"),
    )(qt, k, vt, *extra)


HALO = 8


def _pool_kernel(xp_ref, x_ref, xn_ref, mod_ref, g_ref, o_ref, hc_ref, *, tm, seq):
    i = pl.program_id(1)
    g = g_ref[...]
    shift, scale = mod_ref[0, 3:4, :], mod_ref[0, 4:5, :]
    hc_ref[0:HALO, :] = jnp.where(i > 0, _modulate(xp_ref[0], g, shift, scale), 0.0)
    hc_ref[HALO:HALO + tm, :] = _modulate(x_ref[0], g, shift, scale)
    hc_ref[HALO + tm:, :] = jnp.where(i < pl.num_programs(1) - 1, _modulate(xn_ref[0], g, shift, scale), 0.0)
    gc = x_ref.shape[2] // len(POOL_WINDOWS)
    t = i * tm + lax.broadcasted_iota(jnp.int32, (tm, 1), 0)
    for k, w in enumerate(POOL_WINDOWS):
        cols = slice(k * gc, (k + 1) * gc)
        acc = hc_ref[HALO - w // 2:HALO - w // 2 + tm, cols]
        for s in range(1, w):
            acc = acc + hc_ref[HALO - w // 2 + s:HALO - w // 2 + s + tm, cols]
        cnt = (jnp.minimum(t + w // 2, seq) - jnp.maximum(t - w // 2, 0)).astype(F32)
        o_ref[0, :, cols] = (acc / cnt - hc_ref[HALO:HALO + tm, cols]).astype(BF16)


def _pool_call(x, mod, g, *, mod_base, mod_per_batch, tm):
    b, seq, d = x.shape
    nb = tm // HALO
    last = seq // HALO - 1
    mod_spec = pl.BlockSpec((1, N_MOD, d), lambda b_, i: (mod_base + b_ * mod_per_batch, 0, 0))
    return pl.pallas_call(
        functools.partial(_pool_kernel, tm=tm, seq=seq),
        grid=(b, seq // tm),
        in_specs=[
            pl.BlockSpec((1, HALO, d), lambda b_, i: (b_, jnp.maximum(i * nb - 1, 0), 0)),
            pl.BlockSpec((1, tm, d), lambda b_, i: (b_, i, 0)),
            pl.BlockSpec((1, HALO, d), lambda b_, i: (b_, jnp.minimum((i + 1) * nb, last), 0)),
            mod_spec,
            pl.BlockSpec((1, d), lambda b_, i: (0, 0)),
        ],
        out_specs=pl.BlockSpec((1, tm, d), lambda b_, i: (b_, i, 0)),
        out_shape=jax.ShapeDtypeStruct((b, seq, d), BF16),
        scratch_shapes=[pltpu.VMEM((tm + 2 * HALO, d), F32)],
        compiler_params=_cparams(("parallel", "parallel")),
        name="pool_mixer",
    )(x, x, x, mod, g.reshape(1, d))


def _rope_tables(seq, dim, pad_lo, pad_hi, reps):
    q = dim // 4
    t = jnp.arange(seq)
    freqs = ROPE_BASE ** (-jnp.arange(q, dtype=F32) / q)
    ar = (t // GRID_W).astype(F32)[:, None] * freqs
    ac = (t % GRID_W).astype(F32)[:, None] * freqs
    cos = jnp.concatenate([jnp.cos(ar), jnp.cos(ar), jnp.cos(ac), jnp.cos(ac)], axis=-1)
    sin = jnp.concatenate([-jnp.sin(ar), jnp.sin(ar), -jnp.sin(ac), jnp.sin(ac)], axis=-1)
    one = lambda w: jnp.ones((seq, w), F32)
    zero = lambda w: jnp.zeros((seq, w), F32)
    cos = jnp.tile(jnp.concatenate([one(pad_lo), cos, one(pad_hi)], axis=-1), (1, reps))
    sin = jnp.tile(jnp.concatenate([zero(pad_lo), sin, zero(pad_hi)], axis=-1), (1, reps))
    return cos, sin


def _group_matrix(sizes):
    m = jnp.zeros((MXU_DIM, MXU_DIM), F32)
    lo = 0
    while lo < MXU_DIM:
        for size, width in sizes:
            if size:
                m = m.at[lo:lo + width, lo:lo + width].set(1.0 / size)
            lo += width
    return m.astype(BF16)


def _attn_params(w_in, qk_g, q_a_g, w_q_b, kv_a_g, w_kv_b, nope_g, rope_g):
    d = w_in.shape[0]
    hq = MLA_NOPE + MLA_ROPE
    pad = LANES - hq
    w_in_p = jnp.concatenate([w_in[:, :IN_KR], jnp.zeros((d, MLA_NOPE), F32), w_in[:, IN_KR:],
                              jnp.zeros((d, pad), F32)], axis=-1)
    wq = w_q_b.reshape(MLA_Q_RANK, MLA_HEADS, hq)
    wq = jnp.concatenate([wq, jnp.zeros((MLA_Q_RANK, MLA_HEADS, pad), F32)], axis=-1)
    wkv = w_kv_b.reshape(MLA_KV_RANK, MLA_HEADS, MLA_NOPE + MLA_V)
    wk = jnp.concatenate([wkv[..., :MLA_NOPE], jnp.zeros((MLA_KV_RANK, MLA_HEADS, LANES - MLA_NOPE), F32)], axis=-1)
    wv = wkv[..., MLA_NOPE:]
    q_scale = hq ** -0.5 * LOG2E
    zeros = lambda w: jnp.zeros((w,), F32)
    gqm = jnp.tile(jnp.concatenate([nope_g[0], rope_g[0], zeros(pad)]), MLA_HEADS) * q_scale
    gkm = jnp.tile(jnp.concatenate([nope_g[1], zeros(LANES - MLA_NOPE)]), MLA_HEADS)
    gkr = jnp.concatenate([zeros(MLA_NOPE), rope_g[1], zeros(pad)])
    row = lambda v: v.reshape(1, -1)
    return {
        "w_in": w_in_p.astype(BF16),
        "w_q_b": wq.reshape(MLA_Q_RANK, MLA_HEADS * LANES).astype(BF16),
        "w_kv_b": jnp.concatenate([wk.reshape(MLA_KV_RANK, -1), wv.reshape(MLA_KV_RANK, -1)], axis=-1).astype(BF16),
        "g64": _group_matrix([(DIFF_HD, DIFF_HD)]),
        "gm": _group_matrix([(MLA_NOPE, MLA_NOPE), (MLA_ROPE, MLA_ROPE), (0, pad)]),
        "gqd": row(jnp.tile(qk_g[0], 2 * DIFF_HEADS) * (DIFF_HD ** -0.5 * LOG2E)),
        "gkd": row(jnp.tile(qk_g[1], 2 * DIFF_HEADS)),
        "gqa": row(q_a_g), "gkva": row(kv_a_g),
        "gqm": row(gqm), "gkm": row(gkm), "gkr": row(gkr),
    }


def _logit_bounds(qk_g, nope_g, rope_g):
    amax = lambda v: jnp.max(jnp.abs(v))
    diff = LOG2E * DIFF_HD ** 0.5 * amax(qk_g[0]) * amax(qk_g[1])
    qn = jnp.sqrt(MLA_NOPE * amax(nope_g[0]) ** 2 + MLA_ROPE * amax(rope_g[0]) ** 2)
    kn = jnp.sqrt(MLA_NOPE * amax(nope_g[1]) ** 2 + MLA_ROPE * amax(rope_g[1]) ** 2)
    mla = LOG2E * (MLA_NOPE + MLA_ROPE) ** -0.5 * qn * kn
    return diff, mla


def _attention(qdt, qmt, kv, dlam, subln_g, lam_init, bounds, *, tq, key_block, n_keys, cpi):
    kd, km, vdt, vmt = kv

    def core(kind, bound, qt, k, vt, extra):
        call = functools.partial(_attn_call, kind, qt=qt, k=k, vt=vt, extra=extra, lam_init=lam_init,
                                 tq=tq, key_block=key_block, n_keys=n_keys, cpi=cpi)
        return lax.cond(bound <= MAX_SAFE_LOGIT, lambda: call(bounded=True), lambda: call(bounded=False))

    od = core("diff", bounds[0], qdt, kd, vdt, (dlam, subln_g.reshape(DIFF_VD, 1)))
    om = core("mla", bounds[1], qmt, km, vmt, ())
    return od, om


def _largest_tile(n, cap, mult):
    best = None
    for t in range(mult, min(n, cap) + 1, mult):
        if n % t == 0:
            best = t
    assert best is not None, (n, cap, mult)
    return best


def kernel(x, c, ctx, c_ctx, mod_w, mod_b, norm_g, ffn_w_gate, ffn_w_up, ffn_w_down, attn_w_in, diff_qk_g,
           diff_lambda, diff_subln_g, mla_q_a_g, mla_w_q_b, mla_kv_a_g, mla_w_kv_b, mla_nope_g, mla_rope_g,
           attn_w_out, pool_w, pool_scale):
    b, s, d = x.shape
    lc = ctx.shape[1]
    depth = mod_w.shape[0]
    assert s % GRID_W == 0 and d == 1024
    assert lc % KEY_CHUNK == 0 and s % lc == 0

    tm_l = _largest_tile(s, 512, KEY_CHUNK)
    tm_c = _largest_tile(lc, 512, KEY_CHUNK)
    tq = _largest_tile(s, 512, 128)
    cpi = _largest_tile((s + lc) // KEY_CHUNK, 11, 1)

    rows = -(-(b + 1) // 8) * 8
    cvec = jnp.concatenate([c, c_ctx[None], jnp.zeros((rows - b - 1, d), F32)], axis=0)
    mods = _mod_call(cvec, mod_w, mod_b).reshape(depth, rows, N_MOD, d)

    wg, wu, wd = ffn_w_gate.astype(BF16), ffn_w_up.astype(BF16), ffn_w_down.astype(BF16)
    cos_d, sin_d = _rope_tables(s, DIFF_HD, 0, 0, LANES // DIFF_HD)
    cos_m, sin_m = _rope_tables(s, MLA_ROPE, MLA_NOPE, LANES - MLA_NOPE - MLA_ROPE, 1)
    tab_l = (cos_d, sin_d, cos_m, sin_m)
    ones, zeros = jnp.ones((lc, LANES), F32), jnp.zeros((lc, LANES), F32)
    tab_c = (ones, zeros, ones, zeros)

    xl = x.reshape(b * s, d)
    xc = ctx.reshape(b * lc, d)
    lat = dict(tiles_per_mod=s // tm_l, mod_base=0, tm=tm_l)
    cx = dict(tiles_per_mod=b * lc // tm_c + 1, mod_base=b, tm=tm_c)

    for layer in range(depth):
        even = layer % 2 == 0
        ctx_out = layer < depth - 1
        ctx_in = ctx_out or even
        i = layer // 2
        g = norm_g[layer]
        mod = mods[layer]
        fw1 = (wg[layer, 0], wu[layer, 0], wd[layer, 0])
        fw2 = (wg[layer, 1], wu[layer, 1], wd[layer, 1])

        xl = _ffn_call(xl, mod, g[0], *fw1, idx=0, **lat)
        if ctx_in:
            xc = _ffn_call(xc, mod, g[0], *fw1, idx=0, **cx)

        if even:
            lam_init = 0.8 - 0.6 * math.exp(-0.3 * layer)
            pw = _attn_params(attn_w_in[i], diff_qk_g[i], mla_q_a_g[i], mla_w_q_b[i], mla_kv_a_g[i],
                              mla_w_kv_b[i], mla_nope_g[i], mla_rope_g[i])
            pl_ = _proj_call(xl, mod, g[1], pw, tab_l, None, seq=s, key_len=s + lc, key_off=0, **lat)
            pc_ = _proj_call(xc, mod, g[1], pw, tab_c, pl_[2:], seq=lc, key_len=s + lc, key_off=s, **cx)
            kv = pc_[2:]
            bounds = _logit_bounds(diff_qk_g[i], mla_nope_g[i], mla_rope_g[i])
            attn = functools.partial(_attention, kv=kv, dlam=diff_lambda[i].astype(F32), subln_g=diff_subln_g[i],
                                     lam_init=lam_init, bounds=bounds)
            od, om = attn(pl_[0], pl_[1], tq=tq, key_block=0, n_keys=s + lc, cpi=cpi)
            wo = attn_w_out[i].astype(BF16)
            mix_l = ("attn", od.reshape(b * s, -1), om.reshape(b * s, -1), wo)
            mix_c = None
            if ctx_out:
                od, om = attn(pc_[0], pc_[1], tq=lc, key_block=s // lc, n_keys=lc, cpi=lc // KEY_CHUNK)
                mix_c = ("attn", od.reshape(b * lc, -1), om.reshape(b * lc, -1), wo)
        else:
            wp = pool_w[i].astype(BF16)
            dl_ = _pool_call(xl.reshape(b, s, d), mod, g[1], mod_base=0, mod_per_batch=1, tm=tm_l)
            mix_l = ("pool", dl_.reshape(b * s, d), wp, pool_scale[i])
            mix_c = None
            if ctx_out:
                dc_ = _pool_call(xc.reshape(b, lc, d), mod, g[1], mod_base=b, mod_per_batch=0, tm=tm_c)
                mix_c = ("pool", dc_.reshape(b * lc, d), wp, pool_scale[i])

        xl = _ffn_call(xl, mod, g[2], *fw2, idx=2, mixer=mix_l, **lat)
        if ctx_out:
            xc = _ffn_call(xc, mod, g[2], *fw2, idx=2, mixer=mix_c, **cx)

    return xl.reshape(b, s, d)
```

```python
import functools
import math

import jax
import jax.numpy as jnp
from jax import lax
from jax.experimental import pallas as pl
from jax.experimental.pallas import tpu as pltpu

F32 = jnp.float32
BF16 = jnp.bfloat16

EPS = 1e-6
ROPE_BASE = 10000.0
GRID_W = 64
N_MOD = 9
DIFF_HEADS = 4
DIFF_HD = 64
DIFF_VD = 128
MLA_HEADS = 8
MLA_NOPE = 64
MLA_ROPE = 32
MLA_V = 64
MLA_Q_RANK = 384
MLA_KV_RANK = 256
POOL_WINDOWS = (2, 4, 8, 16)
LANES = 128
MXU_DIM = 256
KEY_CHUNK = MXU_DIM
MLA_QK_LAG = 4
VMEM_LIMIT = 56 * 1024 * 1024
LOG2E = math.log2(math.e)
MAX_SAFE_LOGIT = 60.0

IN_DQ, IN_DK, IN_DV, IN_CQ, IN_CKV, IN_KR = 0, 512, 1024, 1536, 1920, 2176
IN_W_PAD = 2304


def _cparams(sem):
    return pltpu.CompilerParams(dimension_semantics=sem, vmem_limit_bytes=VMEM_LIMIT)


def _resident(shape):
    nd = len(shape)
    return pl.BlockSpec(shape, lambda *_: (0,) * nd, pipeline_mode=pl.Buffered(1))


def _modulate(x, g, shift, scale):
    r = lax.rsqrt(jnp.mean(x * x, axis=-1, keepdims=True) + EPS)
    return (x * r) * g * (1.0 + scale) + shift


def _mod_kernel(c_ref, w_ref, b_ref, o_ref):
    c = c_ref[...]
    s = (c * (1.0 / (1.0 + jnp.exp(-c)))).astype(BF16)
    o_ref[0] = jnp.dot(s, w_ref[0].astype(BF16), preferred_element_type=F32) + b_ref[0]


def _mod_call(cvec, mod_w, mod_b):
    depth, d, n = mod_w.shape
    rows = cvec.shape[0]
    tn = 1024
    return pl.pallas_call(
        _mod_kernel,
        grid=(depth, n // tn),
        in_specs=[
            pl.BlockSpec((rows, d), lambda l, j: (0, 0)),
            pl.BlockSpec((1, d, tn), lambda l, j: (l, 0, j)),
            pl.BlockSpec((1, 1, tn), lambda l, j: (l, 0, j)),
        ],
        out_specs=pl.BlockSpec((1, rows, tn), lambda l, j: (l, 0, j)),
        out_shape=jax.ShapeDtypeStruct((depth, rows, n), F32),
        compiler_params=_cparams(("parallel", "parallel")),
        name="mod_vectors",
    )(cvec, mod_w, mod_b.reshape(depth, 1, n))


def _ffn_body(x, mod_ref, g_ref, wg_ref, wu_ref, wd_ref, o_ref, idx, fc):
    shift = mod_ref[0, 3 * idx:3 * idx + 1, :]
    scale = mod_ref[0, 3 * idx + 1:3 * idx + 2, :]
    gate = mod_ref[0, 3 * idx + 2:3 * idx + 3, :]
    h = _modulate(x, g_ref[...], shift, scale).astype(BF16)
    d_ff = wg_ref.shape[1]
    acc = jnp.zeros(x.shape, F32)
    for c in range(d_ff // fc):
        sl = slice(c * fc, (c + 1) * fc)
        gt = jnp.dot(h, wg_ref[:, sl], preferred_element_type=F32)
        up = jnp.dot(h, wu_ref[:, sl], preferred_element_type=F32)
        a = (gt * (1.0 / (1.0 + jnp.exp(-gt))) * up).astype(BF16)
        acc = acc + jnp.dot(a, wd_ref[sl, :], preferred_element_type=F32)
    o_ref[...] = x + (0.5 * gate) * acc


def _ffn_kernel(x_ref, mod_ref, g_ref, wg_ref, wu_ref, wd_ref, o_ref, *, idx, fc):
    _ffn_body(x_ref[...], mod_ref, g_ref, wg_ref, wu_ref, wd_ref, o_ref, idx, fc)


def _ffn_attn_kernel(x_ref, ud_ref, um_ref, wo_ref, mod_ref, g_ref, wg_ref, wu_ref, wd_ref, o_ref, *, idx, fc):
    nd = ud_ref.shape[1]
    y = (jnp.dot(ud_ref[...], wo_ref[:nd, :], preferred_element_type=F32)
         + jnp.dot(um_ref[...], wo_ref[nd:, :], preferred_element_type=F32))
    x = x_ref[...] + mod_ref[0, 5:6, :] * y
    _ffn_body(x, mod_ref, g_ref, wg_ref, wu_ref, wd_ref, o_ref, idx, fc)


def _ffn_pool_kernel(x_ref, u_ref, wp_ref, ps_ref, mod_ref, g_ref, wg_ref, wu_ref, wd_ref, o_ref, *, idx, fc):
    gc = wp_ref.shape[1]
    ys = [jnp.dot(u_ref[:, k * gc:(k + 1) * gc], wp_ref[k], preferred_element_type=F32)
          for k in range(wp_ref.shape[0])]
    y = jnp.concatenate(ys, axis=-1) * ps_ref[...]
    x = x_ref[...] + mod_ref[0, 5:6, :] * y
    _ffn_body(x, mod_ref, g_ref, wg_ref, wu_ref, wd_ref, o_ref, idx, fc)


def _ffn_call(x, mod, g, wg, wu, wd, *, idx, tiles_per_mod, mod_base, tm, mixer=None):
    n, d = x.shape
    d_ff = wg.shape[1]
    fc = MXU_DIM
    tok = pl.BlockSpec((tm, d), lambda i: (i, 0))
    mod_spec = pl.BlockSpec((1, N_MOD, d), lambda i: (mod_base + i // tiles_per_mod, 0, 0))
    common_specs = [mod_spec, _resident((1, d)), _resident((d, d_ff)), _resident((d, d_ff)),
                    _resident((d_ff, d))]
    common_args = [mod, g.reshape(1, d), wg, wu, wd]
    if mixer is None:
        kern = functools.partial(_ffn_kernel, idx=idx, fc=fc)
        specs, args = [tok] + common_specs, [x] + common_args
    elif mixer[0] == "attn":
        _, ud, um, wo = mixer
        kern = functools.partial(_ffn_attn_kernel, idx=idx, fc=fc)
        specs = [tok, pl.BlockSpec((tm, ud.shape[1]), lambda i: (i, 0)),
                 pl.BlockSpec((tm, um.shape[1]), lambda i: (i, 0)), _resident(wo.shape)] + common_specs
        args = [x, ud, um, wo] + common_args
    else:
        _, u, wp, ps = mixer
        kern = functools.partial(_ffn_pool_kernel, idx=idx, fc=fc)
        specs = [tok, pl.BlockSpec((tm, u.shape[1]), lambda i: (i, 0)), _resident(wp.shape),
                 _resident((1, d))] + common_specs
        args = [x, u, wp, ps.reshape(1, d)] + common_args
    return pl.pallas_call(
        kern,
        grid=(n // tm,),
        in_specs=specs,
        out_specs=tok,
        out_shape=jax.ShapeDtypeStruct((n, d), F32),
        compiler_params=_cparams(("parallel",)),
        name="ffn_half",
    )(*args)


def _group_rms(x, gm_ref, gain):
    outs = []
    for j in range(x.shape[1] // MXU_DIM):
        xb = x[:, j * MXU_DIM:(j + 1) * MXU_DIM]
        ms = jnp.dot((xb * xb).astype(BF16), gm_ref[...], preferred_element_type=F32)
        outs.append(xb * lax.rsqrt(ms + EPS) * gain[:, j * MXU_DIM:(j + 1) * MXU_DIM])
    return jnp.concatenate(outs, axis=-1)


def _rope(x, cos, sin, half):
    outs = []
    lane = lax.broadcasted_iota(jnp.int32, (x.shape[0], LANES), 1)
    first = (lane % (2 * half)) < half
    for j in range(x.shape[1] // LANES):
        xb = x[:, j * LANES:(j + 1) * LANES]
        partner = jnp.where(first, pltpu.roll(xb, LANES - half, 1), pltpu.roll(xb, half, 1))
        outs.append(xb * cos + partner * sin)
    return jnp.concatenate(outs, axis=-1)


def _store_values_t(v, vt_ref):
    for j in range(v.shape[1] // LANES):
        t = v[:, j * LANES:(j + 1) * LANES].T.astype(BF16)
        for c in range(vt_ref.shape[1]):
            vt_ref[0, c, j * LANES:(j + 1) * LANES, :] = t[:, c * KEY_CHUNK:(c + 1) * KEY_CHUNK]


def _proj_kernel(x_ref, mod_ref, g_ref, win_ref, wqb_ref, wkvb_ref, g64_ref, gm_ref,
                 gqd_ref, gkd_ref, gqa_ref, gkva_ref, gqm_ref, gkm_ref, gkr_ref,
                 cd_ref, sd_ref, cm_ref, sm_ref, *refs):
    qdt_ref, qmt_ref, kd_ref, km_ref, vdt_ref, vmt_ref = refs[-6:]
    x = x_ref[...]
    h = _modulate(x, g_ref[...], mod_ref[0, 3:4, :], mod_ref[0, 4:5, :]).astype(BF16)

    def inp(lo, hi):
        return jnp.dot(h, win_ref[:, lo:hi], preferred_element_type=F32)

    cd, sd, cm, sm = cd_ref[...], sd_ref[...], cm_ref[...], sm_ref[...]

    dq = _rope(_group_rms(inp(IN_DQ, IN_DK), g64_ref, gqd_ref[...]), cd, sd, DIFF_HD // 4)
    row = lax.broadcasted_iota(jnp.int32, (LANES, x.shape[0]), 0)
    for hd in range(DIFF_HEADS):
        t = dq[:, hd * LANES:(hd + 1) * LANES].T
        qdt_ref[0, (2 * hd) * LANES:(2 * hd + 1) * LANES, :] = jnp.where(row < DIFF_HD, t, 0.0).astype(BF16)
        qdt_ref[0, (2 * hd + 1) * LANES:(2 * hd + 2) * LANES, :] = jnp.where(row < DIFF_HD, 0.0, t).astype(BF16)
    dk = _rope(_group_rms(inp(IN_DK, IN_DV), g64_ref, gkd_ref[...]), cd, sd, DIFF_HD // 4)
    kd_ref[0] = dk.astype(BF16)
    _store_values_t(inp(IN_DV, IN_CQ), vdt_ref)

    cq = inp(IN_CQ, IN_CKV)
    cq = cq * lax.rsqrt(jnp.mean(cq * cq, axis=-1, keepdims=True) + EPS) * gqa_ref[...]
    mq = jnp.dot(cq.astype(BF16), wqb_ref[...], preferred_element_type=F32)
    mq = _rope(_group_rms(mq, gm_ref, gqm_ref[...]), cm, sm, MLA_ROPE // 4)
    for hd in range(MLA_HEADS):
        qmt_ref[0, hd * LANES:(hd + 1) * LANES, :] = mq[:, hd * LANES:(hd + 1) * LANES].T.astype(BF16)

    ckv = inp(IN_CKV, IN_KR)
    ckv = ckv * lax.rsqrt(jnp.mean(ckv * ckv, axis=-1, keepdims=True) + EPS) * gkva_ref[...]
    kv = jnp.dot(ckv.astype(BF16), wkvb_ref[...], preferred_element_type=F32)
    n_k = MLA_HEADS * LANES
    kn = _group_rms(kv[:, :n_k], gm_ref, gkm_ref[...])
    _store_values_t(kv[:, n_k:], vmt_ref)

    kr_raw = inp(IN_KR, IN_W_PAD)
    kr2 = jnp.concatenate([kr_raw, kr_raw], axis=-1)
    kr = _group_rms(kr2, gm_ref, jnp.concatenate([gkr_ref[...], gkr_ref[...]], axis=-1))[:, :LANES]
    kr = _rope(kr, cm, sm, MLA_ROPE // 4)
    km_ref[0] = (kn + jnp.concatenate([kr] * MLA_HEADS, axis=-1)).astype(BF16)


def _proj_call(x, mod, g, pw, tables, kv_bufs, *, seq, key_len, key_off, tiles_per_mod, mod_base, tm):
    n, d = x.shape
    b = n // seq
    tps = seq // tm
    cpt = tm // KEY_CHUNK
    assert key_off % tm == 0 and tm % KEY_CHUNK == 0
    tab = pl.BlockSpec((tm, LANES), lambda i: (i % tps, 0))
    mod_spec = pl.BlockSpec((1, N_MOD, d), lambda i: (mod_base + i // tiles_per_mod, 0, 0))
    consts = [pw["w_in"], pw["w_q_b"], pw["w_kv_b"], pw["g64"], pw["gm"],
              pw["gqd"], pw["gkd"], pw["gqa"], pw["gkva"], pw["gqm"], pw["gkm"], pw["gkr"]]
    qrows = 2 * DIFF_HEADS * LANES
    qspec = pl.BlockSpec((1, qrows, tm), lambda i: (i // tps, 0, i % tps))
    kspec = lambda w: pl.BlockSpec((1, tm, w), lambda i: (i // tps, key_off // tm + i % tps, 0))
    vspec = pl.BlockSpec((1, cpt, 512, KEY_CHUNK), lambda i: (i // tps, key_off // tm + i % tps, 0, 0))
    nkc = key_len // KEY_CHUNK
    out_shape = [jax.ShapeDtypeStruct((b, qrows, seq), BF16), jax.ShapeDtypeStruct((b, qrows, seq), BF16),
                 jax.ShapeDtypeStruct((b, key_len, 512), BF16), jax.ShapeDtypeStruct((b, key_len, qrows), BF16),
                 jax.ShapeDtypeStruct((b, nkc, 512, KEY_CHUNK), BF16),
                 jax.ShapeDtypeStruct((b, nkc, 512, KEY_CHUNK), BF16)]
    in_specs = ([pl.BlockSpec((tm, d), lambda i: (i, 0)), mod_spec, _resident((1, d))]
                + [_resident(c.shape) for c in consts] + [tab] * 4)
    args = [x, mod, g.reshape(1, d), *consts, *tables]
    aliases = {}
    if kv_bufs is not None:
        aliases = {len(args) + j: 2 + j for j in range(4)}
        in_specs += [pl.BlockSpec(memory_space=pl.ANY)] * 4
        args += list(kv_bufs)
    return pl.pallas_call(
        _proj_kernel,
        grid=(n // tm,),
        in_specs=in_specs,
        out_specs=[qspec, qspec, kspec(512), kspec(qrows), vspec, vspec],
        out_shape=out_shape,
        input_output_aliases=aliases,
        compiler_params=_cparams(("parallel",)),
        name="attn_proj",
    )(*args)


def _online_step(s, vt, m_ref, l_ref, acc_ref, k):
    m_old = m_ref[k:k + 1, :]
    m_new = jnp.maximum(m_old, jnp.max(s, axis=0, keepdims=True))
    alpha = jnp.exp2(m_old - m_new)
    p = jnp.exp2(s - m_new)
    l_ref[k:k + 1, :] = alpha * l_ref[k:k + 1, :] + jnp.sum(p, axis=0, keepdims=True)
    acc_ref[k] = alpha * acc_ref[k] + jnp.dot(vt, p.astype(BF16), preferred_element_type=F32)
    m_ref[k:k + 1, :] = m_new


def _init_stats(m_ref, l_ref, acc_ref):
    m_ref[...] = jnp.full(m_ref.shape, -jnp.inf, F32)
    l_ref[...] = jnp.zeros(l_ref.shape, F32)
    acc_ref[...] = jnp.zeros(acc_ref.shape, F32)


def _diff_finalize(lam_ref, sg_ref, o_ref, l_ref, acc_ref, lam_init):
    dl = lam_ref[...]
    lam = (jnp.exp(jnp.sum(dl[0:1] * dl[1:2], keepdims=True))
           - jnp.exp(jnp.sum(dl[2:3] * dl[3:4], keepdims=True)) + lam_init)
    for hd in range(DIFF_HEADS):
        o1 = acc_ref[2 * hd] / l_ref[2 * hd:2 * hd + 1, :]
        o2 = acc_ref[2 * hd + 1] / l_ref[2 * hd + 1:2 * hd + 2, :]
        o = o1 - lam * o2
        r = lax.rsqrt(jnp.mean(o * o, axis=0, keepdims=True) + EPS)
        o = (o * r) * sg_ref[...] * (1.0 - lam_init)
        o_ref[0, :, hd * DIFF_VD:(hd + 1) * DIFF_VD] = o.T.astype(BF16)


def _mla_finalize(o_ref, l_ref, acc_ref):
    for pair in range(MLA_HEADS // 2):
        o = jnp.concatenate([acc_ref[2 * pair + j] / l_ref[2 * pair + j:2 * pair + j + 1, :] for j in range(2)],
                            axis=0)
        o_ref[0, :, pair * LANES:(pair + 1) * LANES] = o.T.astype(BF16)


def _online_kernel(qt_ref, k_ref, vt_ref, *refs, n_heads, heads_per_kv, dv, finalize):
    m_ref, l_ref, acc_ref = refs[-3:]
    j = pl.program_id(2)

    @pl.when(j == 0)
    def _():
        _init_stats(m_ref, l_ref, acc_ref)

    for k in range(n_heads):
        kv = k // heads_per_kv
        s = jnp.dot(k_ref[0, :, kv * LANES:(kv + 1) * LANES], qt_ref[0, k * LANES:(k + 1) * LANES, :],
                    preferred_element_type=F32)
        _online_step(s, vt_ref[0, 0, kv * dv:(kv + 1) * dv, :], m_ref, l_ref, acc_ref, k)

    @pl.when(j == pl.num_programs(2) - 1)
    def _():
        finalize(*refs[:-3], l_ref, acc_ref)


def _bounded_kernel(qt_ref, k_ref, vt_ref, *refs, n_heads, heads_per_kv, dv, cpi, qk_lag, finalize):
    l_ref, acc_ref = refs[-2:]
    l_ref[...] = jnp.zeros(l_ref.shape, F32)
    acc_ref[...] = jnp.zeros(acc_ref.shape, F32)
    rows = cpi * KEY_CHUNK
    lag = cpi if qk_lag is None else min(qk_lag, cpi)

    def body(it, carry):
        r0 = pl.multiple_of(it * rows, KEY_CHUNK)
        for k in range(n_heads):
            kv = k // heads_per_kv
            qt = qt_ref[0, k * LANES:(k + 1) * LANES, :]

            def keys(lo, n):
                return k_ref[0, pl.ds(r0 + lo * KEY_CHUNK, n * KEY_CHUNK), kv * LANES:(kv + 1) * LANES]

            s, pv, ls = {}, None, None
            if qk_lag is None:
                whole = jnp.dot(keys(0, cpi), qt, preferred_element_type=F32)
                s = {u: whole[u * KEY_CHUNK:(u + 1) * KEY_CHUNK] for u in range(cpi)}
            for u in range(cpi + lag):
                if u < cpi and qk_lag is not None:
                    s[u] = jnp.dot(keys(u, 1), qt, preferred_element_type=F32)
                if u >= lag:
                    p = jnp.exp2(s.pop(u - lag))
                    psum = jnp.sum(p, axis=0, keepdims=True)
                    d = jnp.dot(vt_ref[0, it * cpi + u - lag, kv * dv:(kv + 1) * dv, :], p.astype(BF16),
                                preferred_element_type=F32)
                    pv = d if pv is None else pv + d
                    ls = psum if ls is None else ls + psum
            l_ref[k:k + 1, :] += ls
            acc_ref[k] += pv
        return carry

    lax.fori_loop(0, vt_ref.shape[1] // cpi, body, 0)
    finalize(*refs[:-2], l_ref, acc_ref)


def _attn_call(kind, bounded, qt, k, vt, extra, lam_init, *, tq, key_block, n_keys, cpi):
    b, _, sq = qt.shape
    nkc = n_keys // KEY_CHUNK
    if kind == "diff":
        dims = dict(n_heads=2 * DIFF_HEADS, heads_per_kv=2, dv=DIFF_VD,
                    finalize=functools.partial(_diff_finalize, lam_init=lam_init))
        extra_specs = [pl.BlockSpec(extra[0].shape, lambda *_: (0, 0)), pl.BlockSpec((DIFF_VD, 1), lambda *_: (0, 0))]
    else:
        dims = dict(n_heads=MLA_HEADS, heads_per_kv=1, dv=MLA_V, finalize=_mla_finalize)
        extra_specs = []
    n_heads, dv = dims["n_heads"], dims["dv"]
    stats = [pltpu.VMEM((n_heads, tq), F32)] * (1 if bounded else 2)
    scratch = stats + [pltpu.VMEM((n_heads, dv, tq), F32)]
    vrows = vt.shape[2]
    out_shape = jax.ShapeDtypeStruct((b, sq, vrows), BF16)
    if bounded:
        body = functools.partial(_bounded_kernel, cpi=cpi, qk_lag=None if kind == "diff" else MLA_QK_LAG, **dims)
        grid = (b, sq // tq)
        in_specs = [
            pl.BlockSpec((1, qt.shape[1], tq), lambda b_, i: (b_, 0, i)),
            pl.BlockSpec((1, n_keys, k.shape[2]), lambda b_, i: (b_, key_block, 0), pipeline_mode=pl.Buffered(1)),
            pl.BlockSpec((1, nkc, vrows, KEY_CHUNK), lambda b_, i: (b_, key_block, 0, 0),
                         pipeline_mode=pl.Buffered(1)),
        ]
        out_spec = pl.BlockSpec((1, tq, vrows), lambda b_, i: (b_, i, 0))
        sem = ("parallel", "arbitrary")
    else:
        body = functools.partial(_online_kernel, **dims)
        grid = (b, sq // tq, nkc)
        in_specs = [
            pl.BlockSpec((1, qt.shape[1], tq), lambda b_, i, j: (b_, 0, i)),
            pl.BlockSpec((1, KEY_CHUNK, k.shape[2]), lambda b_, i, j: (b_, key_block * nkc + j, 0)),
            pl.BlockSpec((1, 1, vrows, KEY_CHUNK), lambda b_, i, j: (b_, key_block * nkc + j, 0, 0)),
        ]
        out_spec = pl.BlockSpec((1, tq, vrows), lambda b_, i, j: (b_, i, 0))
        sem = ("parallel", "parallel", "arbitrary")
    return pl.pallas_call(
        body, grid=grid, in_specs=in_specs + extra_specs, out_specs=out_spec, out_shape=out_shape,
        scratch_shapes=scratch, compiler_params=_cparams(sem),
        name=kind + ("_attn_bounded" if bounded else "_attn"),
    )(qt, k, vt, *extra)


HALO = 8
POOL_ROW_BLOCK = 64


def _pool_kernel(xp_ref, x_ref, xn_ref, mod_ref, g_ref, o_ref, hc_ref, *, tm, seq):
    i = pl.program_id(1)
    g = g_ref[...]
    shift, scale = mod_ref[0, 3:4, :], mod_ref[0, 4:5, :]
    hc_ref[0:HALO, :] = jnp.where(i > 0, _modulate(xp_ref[0], g, shift, scale), 0.0)
    hc_ref[HALO:HALO + tm, :] = _modulate(x_ref[0], g, shift, scale)
    hc_ref[HALO + tm:, :] = jnp.where(i < pl.num_programs(1) - 1, _modulate(xn_ref[0], g, shift, scale), 0.0)
    gc = x_ref.shape[2] // len(POOL_WINDOWS)
    rb = POOL_ROW_BLOCK
    for r0 in range(0, tm, rb):
        t = i * tm + r0 + lax.broadcasted_iota(jnp.int32, (rb, 1), 0)
        for k, w in enumerate(POOL_WINDOWS):
            cols = slice(k * gc, (k + 1) * gc)
            lo = HALO + r0 - w // 2
            acc = hc_ref[lo:lo + rb, cols]
            for s in range(1, w):
                acc = acc + hc_ref[lo + s:lo + s + rb, cols]
            cnt = (jnp.minimum(t + w // 2, seq) - jnp.maximum(t - w // 2, 0)).astype(F32)
            o_ref[0, r0:r0 + rb, cols] = (acc / cnt - hc_ref[HALO + r0:HALO + r0 + rb, cols]).astype(BF16)


def _pool_call(x, mod, g, *, mod_base, mod_per_batch, tm):
    b, seq, d = x.shape
    nb = tm // HALO
    last = seq // HALO - 1
    mod_spec = pl.BlockSpec((1, N_MOD, d), lambda b_, i: (mod_base + b_ * mod_per_batch, 0, 0))
    return pl.pallas_call(
        functools.partial(_pool_kernel, tm=tm, seq=seq),
        grid=(b, seq // tm),
        in_specs=[
            pl.BlockSpec((1, HALO, d), lambda b_, i: (b_, jnp.maximum(i * nb - 1, 0), 0)),
            pl.BlockSpec((1, tm, d), lambda b_, i: (b_, i, 0)),
            pl.BlockSpec((1, HALO, d), lambda b_, i: (b_, jnp.minimum((i + 1) * nb, last), 0)),
            mod_spec,
            pl.BlockSpec((1, d), lambda b_, i: (0, 0)),
        ],
        out_specs=pl.BlockSpec((1, tm, d), lambda b_, i: (b_, i, 0)),
        out_shape=jax.ShapeDtypeStruct((b, seq, d), BF16),
        scratch_shapes=[pltpu.VMEM((tm + 2 * HALO, d), F32)],
        compiler_params=_cparams(("parallel", "parallel")),
        name="pool_mixer",
    )(x, x, x, mod, g.reshape(1, d))


def _rope_tables(seq, dim, pad_lo, pad_hi, reps):
    q = dim // 4
    t = jnp.arange(seq)
    freqs = ROPE_BASE ** (-jnp.arange(q, dtype=F32) / q)
    ar = (t // GRID_W).astype(F32)[:, None] * freqs
    ac = (t % GRID_W).astype(F32)[:, None] * freqs
    cos = jnp.concatenate([jnp.cos(ar), jnp.cos(ar), jnp.cos(ac), jnp.cos(ac)], axis=-1)
    sin = jnp.concatenate([-jnp.sin(ar), jnp.sin(ar), -jnp.sin(ac), jnp.sin(ac)], axis=-1)
    one = lambda w: jnp.ones((seq, w), F32)
    zero = lambda w: jnp.zeros((seq, w), F32)
    cos = jnp.tile(jnp.concatenate([one(pad_lo), cos, one(pad_hi)], axis=-1), (1, reps))
    sin = jnp.tile(jnp.concatenate([zero(pad_lo), sin, zero(pad_hi)], axis=-1), (1, reps))
    return cos, sin


def _group_matrix(sizes):
    m = jnp.zeros((MXU_DIM, MXU_DIM), F32)
    lo = 0
    while lo < MXU_DIM:
        for size, width in sizes:
            if size:
                m = m.at[lo:lo + width, lo:lo + width].set(1.0 / size)
            lo += width
    return m.astype(BF16)


def _attn_params(w_in, qk_g, q_a_g, w_q_b, kv_a_g, w_kv_b, nope_g, rope_g):
    d = w_in.shape[0]
    hq = MLA_NOPE + MLA_ROPE
    pad = LANES - hq
    w_in_p = jnp.concatenate([w_in[:, :IN_KR], jnp.zeros((d, MLA_NOPE), F32), w_in[:, IN_KR:],
                              jnp.zeros((d, pad), F32)], axis=-1)
    wq = w_q_b.reshape(MLA_Q_RANK, MLA_HEADS, hq)
    wq = jnp.concatenate([wq, jnp.zeros((MLA_Q_RANK, MLA_HEADS, pad), F32)], axis=-1)
    wkv = w_kv_b.reshape(MLA_KV_RANK, MLA_HEADS, MLA_NOPE + MLA_V)
    wk = jnp.concatenate([wkv[..., :MLA_NOPE], jnp.zeros((MLA_KV_RANK, MLA_HEADS, LANES - MLA_NOPE), F32)], axis=-1)
    wv = wkv[..., MLA_NOPE:]
    q_scale = hq ** -0.5 * LOG2E
    zeros = lambda w: jnp.zeros((w,), F32)
    gqm = jnp.tile(jnp.concatenate([nope_g[0], rope_g[0], zeros(pad)]), MLA_HEADS) * q_scale
    gkm = jnp.tile(jnp.concatenate([nope_g[1], zeros(LANES - MLA_NOPE)]), MLA_HEADS)
    gkr = jnp.concatenate([zeros(MLA_NOPE), rope_g[1], zeros(pad)])
    row = lambda v: v.reshape(1, -1)
    return {
        "w_in": w_in_p.astype(BF16),
        "w_q_b": wq.reshape(MLA_Q_RANK, MLA_HEADS * LANES).astype(BF16),
        "w_kv_b": jnp.concatenate([wk.reshape(MLA_KV_RANK, -1), wv.reshape(MLA_KV_RANK, -1)], axis=-1).astype(BF16),
        "g64": _group_matrix([(DIFF_HD, DIFF_HD)]),
        "gm": _group_matrix([(MLA_NOPE, MLA_NOPE), (MLA_ROPE, MLA_ROPE), (0, pad)]),
        "gqd": row(jnp.tile(qk_g[0], 2 * DIFF_HEADS) * (DIFF_HD ** -0.5 * LOG2E)),
        "gkd": row(jnp.tile(qk_g[1], 2 * DIFF_HEADS)),
        "gqa": row(q_a_g), "gkva": row(kv_a_g),
        "gqm": row(gqm), "gkm": row(gkm), "gkr": row(gkr),
    }


def _logit_bounds(qk_g, nope_g, rope_g):
    amax = lambda v: jnp.max(jnp.abs(v))
    diff = LOG2E * DIFF_HD ** 0.5 * amax(qk_g[0]) * amax(qk_g[1])
    qn = jnp.sqrt(MLA_NOPE * amax(nope_g[0]) ** 2 + MLA_ROPE * amax(rope_g[0]) ** 2)
    kn = jnp.sqrt(MLA_NOPE * amax(nope_g[1]) ** 2 + MLA_ROPE * amax(rope_g[1]) ** 2)
    mla = LOG2E * (MLA_NOPE + MLA_ROPE) ** -0.5 * qn * kn
    return diff, mla


def _attention(qdt, qmt, kv, dlam, subln_g, lam_init, bounds, *, tq, key_block, n_keys, cpi):
    kd, km, vdt, vmt = kv

    def core(kind, bound, qt, k, vt, extra):
        call = functools.partial(_attn_call, kind, qt=qt, k=k, vt=vt, extra=extra, lam_init=lam_init,
                                 tq=tq, key_block=key_block, n_keys=n_keys, cpi=cpi)
        return lax.cond(bound <= MAX_SAFE_LOGIT, lambda: call(bounded=True), lambda: call(bounded=False))

    od = core("diff", bounds[0], qdt, kd, vdt, (dlam, subln_g.reshape(DIFF_VD, 1)))
    om = core("mla", bounds[1], qmt, km, vmt, ())
    return od, om


def _largest_tile(n, cap, mult):
    best = None
    for t in range(mult, min(n, cap) + 1, mult):
        if n % t == 0:
            best = t
    assert best is not None, (n, cap, mult)
    return best


def kernel(x, c, ctx, c_ctx, mod_w, mod_b, norm_g, ffn_w_gate, ffn_w_up, ffn_w_down, attn_w_in, diff_qk_g,
           diff_lambda, diff_subln_g, mla_q_a_g, mla_w_q_b, mla_kv_a_g, mla_w_kv_b, mla_nope_g, mla_rope_g,
           attn_w_out, pool_w, pool_scale):
    b, s, d = x.shape
    lc = ctx.shape[1]
    depth = mod_w.shape[0]
    assert s % GRID_W == 0 and d == 1024
    assert lc % KEY_CHUNK == 0 and s % lc == 0

    tm_l = _largest_tile(s, 512, KEY_CHUNK)
    tm_c = _largest_tile(lc, 512, KEY_CHUNK)
    tq = _largest_tile(s, 512, 128)
    cpi = _largest_tile((s + lc) // KEY_CHUNK, 11, 1)

    rows = -(-(b + 1) // 8) * 8
    cvec = jnp.concatenate([c, c_ctx[None], jnp.zeros((rows - b - 1, d), F32)], axis=0)
    mods = _mod_call(cvec, mod_w, mod_b).reshape(depth, rows, N_MOD, d)

    wg, wu, wd = ffn_w_gate.astype(BF16), ffn_w_up.astype(BF16), ffn_w_down.astype(BF16)
    cos_d, sin_d = _rope_tables(s, DIFF_HD, 0, 0, LANES // DIFF_HD)
    cos_m, sin_m = _rope_tables(s, MLA_ROPE, MLA_NOPE, LANES - MLA_NOPE - MLA_ROPE, 1)
    tab_l = (cos_d, sin_d, cos_m, sin_m)
    ones, zeros = jnp.ones((lc, LANES), F32), jnp.zeros((lc, LANES), F32)
    tab_c = (ones, zeros, ones, zeros)

    xl = x.reshape(b * s, d)
    xc = ctx.reshape(b * lc, d)
    lat = dict(tiles_per_mod=s // tm_l, mod_base=0, tm=tm_l)
    cx = dict(tiles_per_mod=b * lc // tm_c + 1, mod_base=b, tm=tm_c)

    for layer in range(depth):
        even = layer % 2 == 0
        ctx_out = layer < depth - 1
        ctx_in = ctx_out or even
        i = layer // 2
        g = norm_g[layer]
        mod = mods[layer]
        fw1 = (wg[layer, 0], wu[layer, 0], wd[layer, 0])
        fw2 = (wg[layer, 1], wu[layer, 1], wd[layer, 1])

        xl = _ffn_call(xl, mod, g[0], *fw1, idx=0, **lat)
        if ctx_in:
            xc = _ffn_call(xc, mod, g[0], *fw1, idx=0, **cx)

        if even:
            lam_init = 0.8 - 0.6 * math.exp(-0.3 * layer)
            pw = _attn_params(attn_w_in[i], diff_qk_g[i], mla_q_a_g[i], mla_w_q_b[i], mla_kv_a_g[i],
                              mla_w_kv_b[i], mla_nope_g[i], mla_rope_g[i])
            pl_ = _proj_call(xl, mod, g[1], pw, tab_l, None, seq=s, key_len=s + lc, key_off=0, **lat)
            pc_ = _proj_call(xc, mod, g[1], pw, tab_c, pl_[2:], seq=lc, key_len=s + lc, key_off=s, **cx)
            kv = pc_[2:]
            bounds = _logit_bounds(diff_qk_g[i], mla_nope_g[i], mla_rope_g[i])
            attn = functools.partial(_attention, kv=kv, dlam=diff_lambda[i].astype(F32), subln_g=diff_subln_g[i],
                                     lam_init=lam_init, bounds=bounds)
            od, om = attn(pl_[0], pl_[1], tq=tq, key_block=0, n_keys=s + lc, cpi=cpi)
            wo = attn_w_out[i].astype(BF16)
            mix_l = ("attn", od.reshape(b * s, -1), om.reshape(b * s, -1), wo)
            mix_c = None
            if ctx_out:
                od, om = attn(pc_[0], pc_[1], tq=lc, key_block=s // lc, n_keys=lc, cpi=lc // KEY_CHUNK)
                mix_c = ("attn", od.reshape(b * lc, -1), om.reshape(b * lc, -1), wo)
        else:
            wp = pool_w[i].astype(BF16)
            dl_ = _pool_call(xl.reshape(b, s, d), mod, g[1], mod_base=0, mod_per_batch=1, tm=tm_l)
            mix_l = ("pool", dl_.reshape(b * s, d), wp, pool_scale[i])
            mix_c = None
            if ctx_out:
                dc_ = _pool_call(xc.reshape(b, lc, d), mod, g[1], mod_base=b, mod_per_batch=0, tm=tm_c)
                mix_c = ("pool", dc_.reshape(b * lc, d), wp, pool_scale[i])

        xl = _ffn_call(xl, mod, g[2], *fw2, idx=2, mixer=mix_l, **lat)
        if ctx_out:
            xc = _ffn_call(xc, mod, g[2], *fw2, idx=2, mixer=mix_c, **cx)

    return xl.reshape(b, s, d)
```

```python
import functools
import math

import jax
import jax.numpy as jnp
from jax import lax
from jax.experimental import pallas as pl
from jax.experimental.pallas import tpu as pltpu

F32 = jnp.float32
BF16 = jnp.bfloat16

EPS = 1e-6
ROPE_BASE = 10000.0
GRID_W = 64
N_MOD = 9
DIFF_HEADS = 4
DIFF_HD = 64
DIFF_VD = 128
MLA_HEADS = 8
MLA_NOPE = 64
MLA_ROPE = 32
MLA_V = 64
MLA_Q_RANK = 384
MLA_KV_RANK = 256
POOL_WINDOWS = (2, 4, 8, 16)
LANES = 128
MXU_DIM = 256
KEY_CHUNK = MXU_DIM
MLA_QK_LAG = 3
KV_DOUBLE_BUFFER_BYTES = 9 * 1024 * 1024
VMEM_LIMIT = 56 * 1024 * 1024
LOG2E = math.log2(math.e)
MAX_SAFE_LOGIT = 60.0

IN_DQ, IN_DK, IN_DV, IN_CQ, IN_CKV, IN_KR = 0, 512, 1024, 1536, 1920, 2176
IN_W_PAD = 2304


def _cparams(sem):
    return pltpu.CompilerParams(dimension_semantics=sem, vmem_limit_bytes=VMEM_LIMIT)


def _resident(shape):
    nd = len(shape)
    return pl.BlockSpec(shape, lambda *_: (0,) * nd, pipeline_mode=pl.Buffered(1))


def _modulate(x, g, shift, scale):
    r = lax.rsqrt(jnp.mean(x * x, axis=-1, keepdims=True) + EPS)
    return (x * r) * g * (1.0 + scale) + shift


def _mod_kernel(c_ref, w_ref, b_ref, o_ref):
    c = c_ref[...]
    s = (c * (1.0 / (1.0 + jnp.exp(-c)))).astype(BF16)
    o_ref[0] = jnp.dot(s, w_ref[0].astype(BF16), preferred_element_type=F32) + b_ref[0]


def _mod_call(cvec, mod_w, mod_b):
    depth, d, n = mod_w.shape
    rows = cvec.shape[0]
    tn = 1024
    return pl.pallas_call(
        _mod_kernel,
        grid=(depth, n // tn),
        in_specs=[
            pl.BlockSpec((rows, d), lambda l, j: (0, 0)),
            pl.BlockSpec((1, d, tn), lambda l, j: (l, 0, j)),
            pl.BlockSpec((1, 1, tn), lambda l, j: (l, 0, j)),
        ],
        out_specs=pl.BlockSpec((1, rows, tn), lambda l, j: (l, 0, j)),
        out_shape=jax.ShapeDtypeStruct((depth, rows, n), F32),
        compiler_params=_cparams(("parallel", "parallel")),
        name="mod_vectors",
    )(cvec, mod_w, mod_b.reshape(depth, 1, n))


def _ffn_body(x, mod_ref, g_ref, wg_ref, wu_ref, wd_ref, o_ref, idx, fc):
    shift = mod_ref[0, 3 * idx:3 * idx + 1, :]
    scale = mod_ref[0, 3 * idx + 1:3 * idx + 2, :]
    gate = mod_ref[0, 3 * idx + 2:3 * idx + 3, :]
    h = _modulate(x, g_ref[...], shift, scale).astype(BF16)
    d_ff = wg_ref.shape[1]
    acc = jnp.zeros(x.shape, F32)
    for c in range(d_ff // fc):
        sl = slice(c * fc, (c + 1) * fc)
        gt = jnp.dot(h, wg_ref[:, sl], preferred_element_type=F32)
        up = jnp.dot(h, wu_ref[:, sl], preferred_element_type=F32)
        a = (gt * (1.0 / (1.0 + jnp.exp(-gt))) * up).astype(BF16)
        acc = acc + jnp.dot(a, wd_ref[sl, :], preferred_element_type=F32)
    o_ref[...] = x + (0.5 * gate) * acc


def _ffn_kernel(x_ref, mod_ref, g_ref, wg_ref, wu_ref, wd_ref, o_ref, *, idx, fc):
    _ffn_body(x_ref[...], mod_ref, g_ref, wg_ref, wu_ref, wd_ref, o_ref, idx, fc)


def _ffn_attn_kernel(x_ref, ud_ref, um_ref, wo_ref, mod_ref, g_ref, wg_ref, wu_ref, wd_ref, o_ref, *, idx, fc):
    nd = ud_ref.shape[1]
    y = (jnp.dot(ud_ref[...], wo_ref[:nd, :], preferred_element_type=F32)
         + jnp.dot(um_ref[...], wo_ref[nd:, :], preferred_element_type=F32))
    x = x_ref[...] + mod_ref[0, 5:6, :] * y
    _ffn_body(x, mod_ref, g_ref, wg_ref, wu_ref, wd_ref, o_ref, idx, fc)


def _ffn_pool_kernel(x_ref, u_ref, wp_ref, ps_ref, mod_ref, g_ref, wg_ref, wu_ref, wd_ref, o_ref, *, idx, fc):
    gc = wp_ref.shape[1]
    ys = [jnp.dot(u_ref[:, k * gc:(k + 1) * gc], wp_ref[k], preferred_element_type=F32)
          for k in range(wp_ref.shape[0])]
    y = jnp.concatenate(ys, axis=-1) * ps_ref[...]
    x = x_ref[...] + mod_ref[0, 5:6, :] * y
    _ffn_body(x, mod_ref, g_ref, wg_ref, wu_ref, wd_ref, o_ref, idx, fc)


def _ffn_call(x, mod, g, wg, wu, wd, *, layer, half, idx, tiles_per_mod, mod_base, tm, mixer=None):
    n, d = x.shape
    d_ff = wg.shape[3]
    fc = MXU_DIM
    tok = pl.BlockSpec((tm, d), lambda i: (i, 0))
    mod_spec = pl.BlockSpec((1, N_MOD, d), lambda i: (mod_base + i // tiles_per_mod, 0, 0))
    wspec = lambda r, c: pl.BlockSpec((None, None, r, c), lambda i: (layer, half, 0, 0),
                                      pipeline_mode=pl.Buffered(1))
    common_specs = [mod_spec, _resident((1, d)), wspec(d, d_ff), wspec(d, d_ff), wspec(d_ff, d)]
    common_args = [mod, g.reshape(1, d), wg, wu, wd]
    if mixer is None:
        kern = functools.partial(_ffn_kernel, idx=idx, fc=fc)
        specs, args = [tok] + common_specs, [x] + common_args
    elif mixer[0] == "attn":
        _, ud, um, wo = mixer
        kern = functools.partial(_ffn_attn_kernel, idx=idx, fc=fc)
        specs = [tok, pl.BlockSpec((tm, ud.shape[1]), lambda i: (i, 0)),
                 pl.BlockSpec((tm, um.shape[1]), lambda i: (i, 0)), _resident(wo.shape)] + common_specs
        args = [x, ud, um, wo] + common_args
    else:
        _, u, wp, ps = mixer
        kern = functools.partial(_ffn_pool_kernel, idx=idx, fc=fc)
        specs = [tok, pl.BlockSpec((tm, u.shape[1]), lambda i: (i, 0)), _resident(wp.shape),
                 _resident((1, d))] + common_specs
        args = [x, u, wp, ps.reshape(1, d)] + common_args
    return pl.pallas_call(
        kern,
        grid=(n // tm,),
        in_specs=specs,
        out_specs=tok,
        out_shape=jax.ShapeDtypeStruct((n, d), F32),
        compiler_params=_cparams(("parallel",)),
        name="ffn_half",
    )(*args)


def _group_rms(x, gm_ref, gain):
    outs = []
    for j in range(x.shape[1] // MXU_DIM):
        xb = x[:, j * MXU_DIM:(j + 1) * MXU_DIM]
        ms = jnp.dot((xb * xb).astype(BF16), gm_ref[...], preferred_element_type=F32)
        outs.append(xb * lax.rsqrt(ms + EPS) * gain[:, j * MXU_DIM:(j + 1) * MXU_DIM])
    return jnp.concatenate(outs, axis=-1)


def _rope(x, cos, sin, half):
    outs = []
    lane = lax.broadcasted_iota(jnp.int32, (x.shape[0], LANES), 1)
    first = (lane % (2 * half)) < half
    for j in range(x.shape[1] // LANES):
        xb = x[:, j * LANES:(j + 1) * LANES]
        partner = jnp.where(first, pltpu.roll(xb, LANES - half, 1), pltpu.roll(xb, half, 1))
        outs.append(xb * cos + partner * sin)
    return jnp.concatenate(outs, axis=-1)


def _store_values_t(v, vt_ref):
    for j in range(v.shape[1] // LANES):
        t = v[:, j * LANES:(j + 1) * LANES].T.astype(BF16)
        for c in range(vt_ref.shape[1]):
            vt_ref[0, c, j * LANES:(j + 1) * LANES, :] = t[:, c * KEY_CHUNK:(c + 1) * KEY_CHUNK]


def _proj_kernel(x_ref, mod_ref, g_ref, win_ref, wqb_ref, wkvb_ref, g64_ref, gm_ref,
                 gqd_ref, gkd_ref, gqa_ref, gkva_ref, gqm_ref, gkm_ref, gkr_ref,
                 cd_ref, sd_ref, cm_ref, sm_ref, *refs):
    qdt_ref, qmt_ref, kd_ref, km_ref, vdt_ref, vmt_ref = refs[-6:]
    x = x_ref[...]
    h = _modulate(x, g_ref[...], mod_ref[0, 3:4, :], mod_ref[0, 4:5, :]).astype(BF16)

    def inp(lo, hi):
        return jnp.dot(h, win_ref[:, lo:hi], preferred_element_type=F32)

    cd, sd, cm, sm = cd_ref[...], sd_ref[...], cm_ref[...], sm_ref[...]

    dq = _rope(_group_rms(inp(IN_DQ, IN_DK), g64_ref, gqd_ref[...]), cd, sd, DIFF_HD // 4)
    row = lax.broadcasted_iota(jnp.int32, (LANES, x.shape[0]), 0)
    for hd in range(DIFF_HEADS):
        t = dq[:, hd * LANES:(hd + 1) * LANES].T
        qdt_ref[0, (2 * hd) * LANES:(2 * hd + 1) * LANES, :] = jnp.where(row < DIFF_HD, t, 0.0).astype(BF16)
        qdt_ref[0, (2 * hd + 1) * LANES:(2 * hd + 2) * LANES, :] = jnp.where(row < DIFF_HD, 0.0, t).astype(BF16)
    dk = _rope(_group_rms(inp(IN_DK, IN_DV), g64_ref, gkd_ref[...]), cd, sd, DIFF_HD // 4)
    kd_ref[0] = dk.astype(BF16)
    _store_values_t(inp(IN_DV, IN_CQ), vdt_ref)

    cq = inp(IN_CQ, IN_CKV)
    cq = cq * lax.rsqrt(jnp.mean(cq * cq, axis=-1, keepdims=True) + EPS) * gqa_ref[...]
    mq = jnp.dot(cq.astype(BF16), wqb_ref[...], preferred_element_type=F32)
    mq = _rope(_group_rms(mq, gm_ref, gqm_ref[...]), cm, sm, MLA_ROPE // 4)
    for hd in range(MLA_HEADS):
        qmt_ref[0, hd * LANES:(hd + 1) * LANES, :] = mq[:, hd * LANES:(hd + 1) * LANES].T.astype(BF16)

    ckv = inp(IN_CKV, IN_KR)
    ckv = ckv * lax.rsqrt(jnp.mean(ckv * ckv, axis=-1, keepdims=True) + EPS) * gkva_ref[...]
    kv = jnp.dot(ckv.astype(BF16), wkvb_ref[...], preferred_element_type=F32)
    n_k = MLA_HEADS * LANES
    kn = _group_rms(kv[:, :n_k], gm_ref, gkm_ref[...])
    _store_values_t(kv[:, n_k:], vmt_ref)

    kr_raw = inp(IN_KR, IN_W_PAD)
    kr2 = jnp.concatenate([kr_raw, kr_raw], axis=-1)
    kr = _group_rms(kr2, gm_ref, jnp.concatenate([gkr_ref[...], gkr_ref[...]], axis=-1))[:, :LANES]
    kr = _rope(kr, cm, sm, MLA_ROPE // 4)
    km_ref[0] = (kn + jnp.concatenate([kr] * MLA_HEADS, axis=-1)).astype(BF16)


def _proj_call(x, mod, g, pw, tables, kv_bufs, *, seq, key_len, key_off, tiles_per_mod, mod_base, tm):
    n, d = x.shape
    b = n // seq
    tps = seq // tm
    cpt = tm // KEY_CHUNK
    assert key_off % tm == 0 and tm % KEY_CHUNK == 0
    tab = pl.BlockSpec((tm, LANES), lambda i: (i % tps, 0))
    mod_spec = pl.BlockSpec((1, N_MOD, d), lambda i: (mod_base + i // tiles_per_mod, 0, 0))
    consts = [pw["w_in"], pw["w_q_b"], pw["w_kv_b"], pw["g64"], pw["gm"],
              pw["gqd"], pw["gkd"], pw["gqa"], pw["gkva"], pw["gqm"], pw["gkm"], pw["gkr"]]
    qrows = 2 * DIFF_HEADS * LANES
    qspec = pl.BlockSpec((1, qrows, tm), lambda i: (i // tps, 0, i % tps))
    kspec = lambda w: pl.BlockSpec((1, tm, w), lambda i: (i // tps, key_off // tm + i % tps, 0))
    vspec = pl.BlockSpec((1, cpt, 512, KEY_CHUNK), lambda i: (i // tps, key_off // tm + i % tps, 0, 0))
    nkc = key_len // KEY_CHUNK
    out_shape = [jax.ShapeDtypeStruct((b, qrows, seq), BF16), jax.ShapeDtypeStruct((b, qrows, seq), BF16),
                 jax.ShapeDtypeStruct((b, key_len, 512), BF16), jax.ShapeDtypeStruct((b, key_len, qrows), BF16),
                 jax.ShapeDtypeStruct((b, nkc, 512, KEY_CHUNK), BF16),
                 jax.ShapeDtypeStruct((b, nkc, 512, KEY_CHUNK), BF16)]
    in_specs = ([pl.BlockSpec((tm, d), lambda i: (i, 0)), mod_spec, _resident((1, d))]
                + [_resident(c.shape) for c in consts] + [tab] * 4)
    args = [x, mod, g.reshape(1, d), *consts, *tables]
    aliases = {}
    if kv_bufs is not None:
        aliases = {len(args) + j: 2 + j for j in range(4)}
        in_specs += [pl.BlockSpec(memory_space=pl.ANY)] * 4
        args += list(kv_bufs)
    return pl.pallas_call(
        _proj_kernel,
        grid=(n // tm,),
        in_specs=in_specs,
        out_specs=[qspec, qspec, kspec(512), kspec(qrows), vspec, vspec],
        out_shape=out_shape,
        input_output_aliases=aliases,
        compiler_params=_cparams(("parallel",)),
        name="attn_proj",
    )(*args)


def _online_step(s, vt, m_ref, l_ref, acc_ref, k):
    m_old = m_ref[k:k + 1, :]
    m_new = jnp.maximum(m_old, jnp.max(s, axis=0, keepdims=True))
    alpha = jnp.exp2(m_old - m_new)
    p = jnp.exp2(s - m_new)
    l_ref[k:k + 1, :] = alpha * l_ref[k:k + 1, :] + jnp.sum(p, axis=0, keepdims=True)
    acc_ref[k] = alpha * acc_ref[k] + jnp.dot(vt, p.astype(BF16), preferred_element_type=F32)
    m_ref[k:k + 1, :] = m_new


def _init_stats(m_ref, l_ref, acc_ref):
    m_ref[...] = jnp.full(m_ref.shape, -jnp.inf, F32)
    l_ref[...] = jnp.zeros(l_ref.shape, F32)
    acc_ref[...] = jnp.zeros(acc_ref.shape, F32)


def _diff_finalize(lam_ref, sg_ref, o_ref, l_ref, acc_ref, lam_init):
    dl = lam_ref[...]
    lam = (jnp.exp(jnp.sum(dl[0:1] * dl[1:2], keepdims=True))
           - jnp.exp(jnp.sum(dl[2:3] * dl[3:4], keepdims=True)) + lam_init)
    for hd in range(DIFF_HEADS):
        o1 = acc_ref[2 * hd] / l_ref[2 * hd:2 * hd + 1, :]
        o2 = acc_ref[2 * hd + 1] / l_ref[2 * hd + 1:2 * hd + 2, :]
        o = o1 - lam * o2
        r = lax.rsqrt(jnp.mean(o * o, axis=0, keepdims=True) + EPS)
        o = (o * r) * sg_ref[...] * (1.0 - lam_init)
        o_ref[0, :, hd * DIFF_VD:(hd + 1) * DIFF_VD] = o.T.astype(BF16)


def _mla_finalize(o_ref, l_ref, acc_ref):
    for pair in range(MLA_HEADS // 2):
        o = jnp.concatenate([acc_ref[2 * pair + j] / l_ref[2 * pair + j:2 * pair + j + 1, :] for j in range(2)],
                            axis=0)
        o_ref[0, :, pair * LANES:(pair + 1) * LANES] = o.T.astype(BF16)


def _online_kernel(qt_ref, k_ref, vt_ref, *refs, n_heads, heads_per_kv, dv, finalize):
    m_ref, l_ref, acc_ref = refs[-3:]
    j = pl.program_id(2)

    @pl.when(j == 0)
    def _():
        _init_stats(m_ref, l_ref, acc_ref)

    for k in range(n_heads):
        kv = k // heads_per_kv
        s = jnp.dot(k_ref[0, :, kv * LANES:(kv + 1) * LANES], qt_ref[0, k * LANES:(k + 1) * LANES, :],
                    preferred_element_type=F32)
        _online_step(s, vt_ref[0, 0, kv * dv:(kv + 1) * dv, :], m_ref, l_ref, acc_ref, k)

    @pl.when(j == pl.num_programs(2) - 1)
    def _():
        finalize(*refs[:-3], l_ref, acc_ref)


def _bounded_kernel(qt_ref, k_ref, vt_ref, *refs, n_heads, heads_per_kv, dv, cpi, qk_lag, finalize):
    l_ref, acc_ref = refs[-2:]
    l_ref[...] = jnp.zeros(l_ref.shape, F32)
    acc_ref[...] = jnp.zeros(acc_ref.shape, F32)
    rows = cpi * KEY_CHUNK
    lag = cpi if qk_lag is None else min(qk_lag, cpi)

    def body(it, carry):
        r0 = pl.multiple_of(it * rows, KEY_CHUNK)
        for k in range(n_heads):
            kv = k // heads_per_kv
            qt = qt_ref[0, k * LANES:(k + 1) * LANES, :]

            def keys(lo, n):
                return k_ref[0, pl.ds(r0 + lo * KEY_CHUNK, n * KEY_CHUNK), kv * LANES:(kv + 1) * LANES]

            s, pv, ls = {}, None, None
            if qk_lag is None:
                whole = jnp.dot(keys(0, cpi), qt, preferred_element_type=F32)
                s = {u: whole[u * KEY_CHUNK:(u + 1) * KEY_CHUNK] for u in range(cpi)}
            for u in range(cpi + lag):
                if u < cpi and qk_lag is not None:
                    s[u] = jnp.dot(keys(u, 1), qt, preferred_element_type=F32)
                if u >= lag:
                    p = jnp.exp2(s.pop(u - lag))
                    psum = jnp.sum(p, axis=0, keepdims=True)
                    d = jnp.dot(vt_ref[0, it * cpi + u - lag, kv * dv:(kv + 1) * dv, :], p.astype(BF16),
                                preferred_element_type=F32)
                    pv = d if pv is None else pv + d
                    ls = psum if ls is None else ls + psum
            l_ref[k:k + 1, :] += ls
            acc_ref[k] += pv
        return carry

    lax.fori_loop(0, vt_ref.shape[1] // cpi, body, 0)
    finalize(*refs[:-2], l_ref, acc_ref)


def _attn_call(kind, bounded, qt, k, vt, extra, lam_init, *, tq, key_block, n_keys, cpi):
    b, _, sq = qt.shape
    nkc = n_keys // KEY_CHUNK
    if kind == "diff":
        dims = dict(n_heads=2 * DIFF_HEADS, heads_per_kv=2, dv=DIFF_VD,
                    finalize=functools.partial(_diff_finalize, lam_init=lam_init))
        extra_specs = [pl.BlockSpec(extra[0].shape, lambda *_: (0, 0)), pl.BlockSpec((DIFF_VD, 1), lambda *_: (0, 0))]
    else:
        dims = dict(n_heads=MLA_HEADS, heads_per_kv=1, dv=MLA_V, finalize=_mla_finalize)
        extra_specs = []
    n_heads, dv = dims["n_heads"], dims["dv"]
    stats = [pltpu.VMEM((n_heads, tq), F32)] * (1 if bounded else 2)
    scratch = stats + [pltpu.VMEM((n_heads, dv, tq), F32)]
    vrows = vt.shape[2]
    out_shape = jax.ShapeDtypeStruct((b, sq, vrows), BF16)
    if bounded:
        body = functools.partial(_bounded_kernel, cpi=cpi, qk_lag=None if kind == "diff" else MLA_QK_LAG, **dims)
        grid = (b, sq // tq)
        k_bufs = 2 if k.shape[2] * n_keys * 2 <= KV_DOUBLE_BUFFER_BYTES else 1
        in_specs = [
            pl.BlockSpec((1, qt.shape[1], tq), lambda b_, i: (b_, 0, i)),
            pl.BlockSpec((1, n_keys, k.shape[2]), lambda b_, i: (b_, key_block, 0),
                         pipeline_mode=pl.Buffered(k_bufs)),
            pl.BlockSpec((1, nkc, vrows, KEY_CHUNK), lambda b_, i: (b_, key_block, 0, 0)),
        ]
        out_spec = pl.BlockSpec((1, tq, vrows), lambda b_, i: (b_, i, 0))
        sem = ("parallel", "arbitrary")
    else:
        body = functools.partial(_online_kernel, **dims)
        grid = (b, sq // tq, nkc)
        in_specs = [
            pl.BlockSpec((1, qt.shape[1], tq), lambda b_, i, j: (b_, 0, i)),
            pl.BlockSpec((1, KEY_CHUNK, k.shape[2]), lambda b_, i, j: (b_, key_block * nkc + j, 0)),
            pl.BlockSpec((1, 1, vrows, KEY_CHUNK), lambda b_, i, j: (b_, key_block * nkc + j, 0, 0)),
        ]
        out_spec = pl.BlockSpec((1, tq, vrows), lambda b_, i, j: (b_, i, 0))
        sem = ("parallel", "parallel", "arbitrary")
    return pl.pallas_call(
        body, grid=grid, in_specs=in_specs + extra_specs, out_specs=out_spec, out_shape=out_shape,
        scratch_shapes=scratch, compiler_params=_cparams(sem),
        name=kind + ("_attn_bounded" if bounded else "_attn"),
    )(qt, k, vt, *extra)


HALO = 8
POOL_ROW_BLOCK = 64


def _pool_kernel(xp_ref, x_ref, xn_ref, mod_ref, g_ref, o_ref, hc_ref, *, tm, seq):
    i = pl.program_id(1)
    g = g_ref[...]
    shift, scale = mod_ref[0, 3:4, :], mod_ref[0, 4:5, :]
    hc_ref[0:HALO, :] = jnp.where(i > 0, _modulate(xp_ref[0], g, shift, scale), 0.0)
    hc_ref[HALO:HALO + tm, :] = _modulate(x_ref[0], g, shift, scale)
    hc_ref[HALO + tm:, :] = jnp.where(i < pl.num_programs(1) - 1, _modulate(xn_ref[0], g, shift, scale), 0.0)
    gc = x_ref.shape[2] // len(POOL_WINDOWS)
    rb = POOL_ROW_BLOCK
    for r0 in range(0, tm, rb):
        t = i * tm + r0 + lax.broadcasted_iota(jnp.int32, (rb, 1), 0)
        for k, w in enumerate(POOL_WINDOWS):
            cols = slice(k * gc, (k + 1) * gc)
            lo = HALO + r0 - w // 2
            acc = hc_ref[lo:lo + rb, cols]
            for s in range(1, w):
                acc = acc + hc_ref[lo + s:lo + s + rb, cols]
            cnt = (jnp.minimum(t + w // 2, seq) - jnp.maximum(t - w // 2, 0)).astype(F32)
            o_ref[0, r0:r0 + rb, cols] = (acc / cnt - hc_ref[HALO + r0:HALO + r0 + rb, cols]).astype(BF16)


def _pool_call(x, mod, g, *, mod_base, mod_per_batch, tm):
    b, seq, d = x.shape
    nb = tm // HALO
    last = seq // HALO - 1
    mod_spec = pl.BlockSpec((1, N_MOD, d), lambda b_, i: (mod_base + b_ * mod_per_batch, 0, 0))
    return pl.pallas_call(
        functools.partial(_pool_kernel, tm=tm, seq=seq),
        grid=(b, seq // tm),
        in_specs=[
            pl.BlockSpec((1, HALO, d), lambda b_, i: (b_, jnp.maximum(i * nb - 1, 0), 0)),
            pl.BlockSpec((1, tm, d), lambda b_, i: (b_, i, 0)),
            pl.BlockSpec((1, HALO, d), lambda b_, i: (b_, jnp.minimum((i + 1) * nb, last), 0)),
            mod_spec,
            pl.BlockSpec((1, d), lambda b_, i: (0, 0)),
        ],
        out_specs=pl.BlockSpec((1, tm, d), lambda b_, i: (b_, i, 0)),
        out_shape=jax.ShapeDtypeStruct((b, seq, d), BF16),
        scratch_shapes=[pltpu.VMEM((tm + 2 * HALO, d), F32)],
        compiler_params=_cparams(("parallel", "parallel")),
        name="pool_mixer",
    )(x, x, x, mod, g.reshape(1, d))


def _rope_tables(seq, dim, pad_lo, pad_hi, reps):
    q = dim // 4
    t = jnp.arange(seq)
    freqs = ROPE_BASE ** (-jnp.arange(q, dtype=F32) / q)
    ar = (t // GRID_W).astype(F32)[:, None] * freqs
    ac = (t % GRID_W).astype(F32)[:, None] * freqs
    cos = jnp.concatenate([jnp.cos(ar), jnp.cos(ar), jnp.cos(ac), jnp.cos(ac)], axis=-1)
    sin = jnp.concatenate([-jnp.sin(ar), jnp.sin(ar), -jnp.sin(ac), jnp.sin(ac)], axis=-1)
    one = lambda w: jnp.ones((seq, w), F32)
    zero = lambda w: jnp.zeros((seq, w), F32)
    cos = jnp.tile(jnp.concatenate([one(pad_lo), cos, one(pad_hi)], axis=-1), (1, reps))
    sin = jnp.tile(jnp.concatenate([zero(pad_lo), sin, zero(pad_hi)], axis=-1), (1, reps))
    return cos, sin


def _group_matrix(sizes):
    m = jnp.zeros((MXU_DIM, MXU_DIM), F32)
    lo = 0
    while lo < MXU_DIM:
        for size, width in sizes:
            if size:
                m = m.at[lo:lo + width, lo:lo + width].set(1.0 / size)
            lo += width
    return m.astype(BF16)


def _attn_params(w_in, qk_g, q_a_g, w_q_b, kv_a_g, w_kv_b, nope_g, rope_g):
    d = w_in.shape[0]
    hq = MLA_NOPE + MLA_ROPE
    pad = LANES - hq
    w_in_p = jnp.concatenate([w_in[:, :IN_KR], jnp.zeros((d, MLA_NOPE), F32), w_in[:, IN_KR:],
                              jnp.zeros((d, pad), F32)], axis=-1)
    wq = w_q_b.reshape(MLA_Q_RANK, MLA_HEADS, hq)
    wq = jnp.concatenate([wq, jnp.zeros((MLA_Q_RANK, MLA_HEADS, pad), F32)], axis=-1)
    wkv = w_kv_b.reshape(MLA_KV_RANK, MLA_HEADS, MLA_NOPE + MLA_V)
    wk = jnp.concatenate([wkv[..., :MLA_NOPE], jnp.zeros((MLA_KV_RANK, MLA_HEADS, LANES - MLA_NOPE), F32)], axis=-1)
    wv = wkv[..., MLA_NOPE:]
    q_scale = hq ** -0.5 * LOG2E
    zeros = lambda w: jnp.zeros((w,), F32)
    gqm = jnp.tile(jnp.concatenate([nope_g[0], rope_g[0], zeros(pad)]), MLA_HEADS) * q_scale
    gkm = jnp.tile(jnp.concatenate([nope_g[1], zeros(LANES - MLA_NOPE)]), MLA_HEADS)
    gkr = jnp.concatenate([zeros(MLA_NOPE), rope_g[1], zeros(pad)])
    row = lambda v: v.reshape(1, -1)
    return {
        "w_in": w_in_p.astype(BF16),
        "w_q_b": wq.reshape(MLA_Q_RANK, MLA_HEADS * LANES).astype(BF16),
        "w_kv_b": jnp.concatenate([wk.reshape(MLA_KV_RANK, -1), wv.reshape(MLA_KV_RANK, -1)], axis=-1).astype(BF16),
        "g64": _group_matrix([(DIFF_HD, DIFF_HD)]),
        "gm": _group_matrix([(MLA_NOPE, MLA_NOPE), (MLA_ROPE, MLA_ROPE), (0, pad)]),
        "gqd": row(jnp.tile(qk_g[0], 2 * DIFF_HEADS) * (DIFF_HD ** -0.5 * LOG2E)),
        "gkd": row(jnp.tile(qk_g[1], 2 * DIFF_HEADS)),
        "gqa": row(q_a_g), "gkva": row(kv_a_g),
        "gqm": row(gqm), "gkm": row(gkm), "gkr": row(gkr),
    }


def _logit_bounds(qk_g, nope_g, rope_g):
    amax = lambda v: jnp.max(jnp.abs(v))
    diff = LOG2E * DIFF_HD ** 0.5 * amax(qk_g[0]) * amax(qk_g[1])
    qn = jnp.sqrt(MLA_NOPE * amax(nope_g[0]) ** 2 + MLA_ROPE * amax(rope_g[0]) ** 2)
    kn = jnp.sqrt(MLA_NOPE * amax(nope_g[1]) ** 2 + MLA_ROPE * amax(rope_g[1]) ** 2)
    mla = LOG2E * (MLA_NOPE + MLA_ROPE) ** -0.5 * qn * kn
    return diff, mla


def _attention(qdt, qmt, kv, dlam, subln_g, lam_init, bounds, *, tq, key_block, n_keys, cpi):
    kd, km, vdt, vmt = kv

    def core(kind, bound, qt, k, vt, extra):
        call = functools.partial(_attn_call, kind, qt=qt, k=k, vt=vt, extra=extra, lam_init=lam_init,
                                 tq=tq, key_block=key_block, n_keys=n_keys, cpi=cpi)
        return lax.cond(bound <= MAX_SAFE_LOGIT, lambda: call(bounded=True), lambda: call(bounded=False))

    od = core("diff", bounds[0], qdt, kd, vdt, (dlam, subln_g.reshape(DIFF_VD, 1)))
    om = core("mla", bounds[1], qmt, km, vmt, ())
    return od, om


def _largest_tile(n, cap, mult):
    best = None
    for t in range(mult, min(n, cap) + 1, mult):
        if n % t == 0:
            best = t
    assert best is not None, (n, cap, mult)
    return best


def kernel(x, c, ctx, c_ctx, mod_w, mod_b, norm_g, ffn_w_gate, ffn_w_up, ffn_w_down, attn_w_in, diff_qk_g,
           diff_lambda, diff_subln_g, mla_q_a_g, mla_w_q_b, mla_kv_a_g, mla_w_kv_b, mla_nope_g, mla_rope_g,
           attn_w_out, pool_w, pool_scale):
    b, s, d = x.shape
    lc = ctx.shape[1]
    depth = mod_w.shape[0]
    assert s % GRID_W == 0 and d == 1024
    assert lc % KEY_CHUNK == 0 and s % lc == 0

    tm_l = _largest_tile(s, 512, KEY_CHUNK)
    tm_c = _largest_tile(lc, 512, KEY_CHUNK)
    tq = _largest_tile(s, 512, 128)
    cpi = _largest_tile((s + lc) // KEY_CHUNK, 11, 1)

    rows = -(-(b + 1) // 8) * 8
    cvec = jnp.concatenate([c, c_ctx[None], jnp.zeros((rows - b - 1, d), F32)], axis=0)
    mods = _mod_call(cvec, mod_w, mod_b).reshape(depth, rows, N_MOD, d)

    wg, wu, wd = ffn_w_gate.astype(BF16), ffn_w_up.astype(BF16), ffn_w_down.astype(BF16)
    cos_d, sin_d = _rope_tables(s, DIFF_HD, 0, 0, LANES // DIFF_HD)
    cos_m, sin_m = _rope_tables(s, MLA_ROPE, MLA_NOPE, LANES - MLA_NOPE - MLA_ROPE, 1)
    tab_l = (cos_d, sin_d, cos_m, sin_m)
    ones, zeros = jnp.ones((lc, LANES), F32), jnp.zeros((lc, LANES), F32)
    tab_c = (ones, zeros, ones, zeros)

    xl = x.reshape(b * s, d)
    xc = ctx.reshape(b * lc, d)
    lat = dict(tiles_per_mod=s // tm_l, mod_base=0, tm=tm_l)
    cx = dict(tiles_per_mod=b * lc // tm_c + 1, mod_base=b, tm=tm_c)

    for layer in range(depth):
        even = layer % 2 == 0
        ctx_out = layer < depth - 1
        ctx_in = ctx_out or even
        i = layer // 2
        g = norm_g[layer]
        mod = mods[layer]
        ffn = functools.partial(_ffn_call, wg=wg, wu=wu, wd=wd, layer=layer)

        xl = ffn(xl, mod, g[0], half=0, idx=0, **lat)
        if ctx_in:
            xc = ffn(xc, mod, g[0], half=0, idx=0, **cx)

        if even:
            lam_init = 0.8 - 0.6 * math.exp(-0.3 * layer)
            pw = _attn_params(attn_w_in[i], diff_qk_g[i], mla_q_a_g[i], mla_w_q_b[i], mla_kv_a_g[i],
                              mla_w_kv_b[i], mla_nope_g[i], mla_rope_g[i])
            pl_ = _proj_call(xl, mod, g[1], pw, tab_l, None, seq=s, key_len=s + lc, key_off=0, **lat)
            pc_ = _proj_call(xc, mod, g[1], pw, tab_c, pl_[2:], seq=lc, key_len=s + lc, key_off=s, **cx)
            kv = pc_[2:]
            bounds = _logit_bounds(diff_qk_g[i], mla_nope_g[i], mla_rope_g[i])
            attn = functools.partial(_attention, kv=kv, dlam=diff_lambda[i].astype(F32), subln_g=diff_subln_g[i],
                                     lam_init=lam_init, bounds=bounds)
            od, om = attn(pl_[0], pl_[1], tq=tq, key_block=0, n_keys=s + lc, cpi=cpi)
            wo = attn_w_out[i].astype(BF16)
            mix_l = ("attn", od.reshape(b * s, -1), om.reshape(b * s, -1), wo)
            mix_c = None
            if ctx_out:
                od, om = attn(pc_[0], pc_[1], tq=lc, key_block=s // lc, n_keys=lc, cpi=lc // KEY_CHUNK)
                mix_c = ("attn", od.reshape(b * lc, -1), om.reshape(b * lc, -1), wo)
        else:
            wp = pool_w[i].astype(BF16)
            dl_ = _pool_call(xl.reshape(b, s, d), mod, g[1], mod_base=0, mod_per_batch=1, tm=tm_l)
            mix_l = ("pool", dl_.reshape(b * s, d), wp, pool_scale[i])
            mix_c = None
            if ctx_out:
                dc_ = _pool_call(xc.reshape(b, lc, d), mod, g[1], mod_base=b, mod_per_batch=0, tm=tm_c)
                mix_c = ("pool", dc_.reshape(b * lc, d), wp, pool_scale[i])

        xl = ffn(xl, mod, g[2], half=1, idx=2, mixer=mix_l, **lat)
        if ctx_out:
            xc = ffn(xc, mod, g[2], half=1, idx=2, mixer=mix_c, **cx)

    return xl.reshape(b, s, d)
```

```python
import functools
import math

import jax
import jax.numpy as jnp
from jax import lax
from jax.experimental import pallas as pl
from jax.experimental.pallas import tpu as pltpu

F32 = jnp.float32
BF16 = jnp.bfloat16

EPS = 1e-6
ROPE_BASE = 10000.0
GRID_W = 64
N_MOD = 9
DIFF_HEADS = 4
DIFF_HD = 64
DIFF_VD = 128
MLA_HEADS = 8
MLA_NOPE = 64
MLA_ROPE = 32
MLA_V = 64
MLA_Q_RANK = 384
MLA_KV_RANK = 256
POOL_WINDOWS = (2, 4, 8, 16)
LANES = 128
MXU_DIM = 256
KEY_CHUNK = MXU_DIM
MLA_QK_LAG = 3
KV_DOUBLE_BUFFER_BYTES = 9 * 1024 * 1024
VMEM_LIMIT = 56 * 1024 * 1024
LOG2E = math.log2(math.e)
MAX_SAFE_LOGIT = 60.0

IN_DQ, IN_DK, IN_DV, IN_CQ, IN_CKV, IN_KR = 0, 512, 1024, 1536, 1920, 2176
IN_W_PAD = 2304


def _cparams(sem):
    return pltpu.CompilerParams(dimension_semantics=sem, vmem_limit_bytes=VMEM_LIMIT)


def _resident(shape):
    nd = len(shape)
    return pl.BlockSpec(shape, lambda *_: (0,) * nd, pipeline_mode=pl.Buffered(1))


def _modulate(x, g, shift, scale):
    r = lax.rsqrt(jnp.mean(x * x, axis=-1, keepdims=True) + EPS)
    return (x * r) * g * (1.0 + scale) + shift


def _mod_kernel(c_ref, w_ref, b_ref, o_ref):
    c = c_ref[...]
    s = (c * (1.0 / (1.0 + jnp.exp(-c)))).astype(BF16)
    o_ref[0] = jnp.dot(s, w_ref[0].astype(BF16), preferred_element_type=F32) + b_ref[0]


def _mod_call(cvec, mod_w, mod_b):
    depth, d, n = mod_w.shape
    rows = cvec.shape[0]
    tn = 1024
    return pl.pallas_call(
        _mod_kernel,
        grid=(depth, n // tn),
        in_specs=[
            pl.BlockSpec((rows, d), lambda l, j: (0, 0)),
            pl.BlockSpec((1, d, tn), lambda l, j: (l, 0, j)),
            pl.BlockSpec((1, 1, tn), lambda l, j: (l, 0, j)),
        ],
        out_specs=pl.BlockSpec((1, rows, tn), lambda l, j: (l, 0, j)),
        out_shape=jax.ShapeDtypeStruct((depth, rows, n), F32),
        compiler_params=_cparams(("parallel", "parallel")),
        name="mod_vectors",
    )(cvec, mod_w, mod_b.reshape(depth, 1, n))


def _ffn_body(x, mod_ref, g_ref, wg_ref, wu_ref, wd_ref, o_ref, idx, fc):
    shift = mod_ref[0, 3 * idx:3 * idx + 1, :]
    scale = mod_ref[0, 3 * idx + 1:3 * idx + 2, :]
    gate = mod_ref[0, 3 * idx + 2:3 * idx + 3, :]
    h = _modulate(x, g_ref[...], shift, scale).astype(BF16)
    d_ff = wg_ref.shape[1]
    acc = jnp.zeros(x.shape, F32)
    for c in range(d_ff // fc):
        sl = slice(c * fc, (c + 1) * fc)
        gt = jnp.dot(h, wg_ref[:, sl], preferred_element_type=F32)
        up = jnp.dot(h, wu_ref[:, sl], preferred_element_type=F32)
        a = (gt * (1.0 / (1.0 + jnp.exp(-gt))) * up).astype(BF16)
        acc = acc + jnp.dot(a, wd_ref[sl, :], preferred_element_type=F32)
    o_ref[...] = x + (0.5 * gate) * acc


def _ffn_kernel(x_ref, mod_ref, g_ref, wg_ref, wu_ref, wd_ref, o_ref, *, idx, fc):
    _ffn_body(x_ref[...], mod_ref, g_ref, wg_ref, wu_ref, wd_ref, o_ref, idx, fc)


def _ffn_attn_kernel(x_ref, ud_ref, um_ref, wo_ref, mod_ref, g_ref, wg_ref, wu_ref, wd_ref, o_ref, *, idx, fc):
    nd = ud_ref.shape[1]
    y = (jnp.dot(ud_ref[...], wo_ref[:nd, :], preferred_element_type=F32)
         + jnp.dot(um_ref[...], wo_ref[nd:, :], preferred_element_type=F32))
    x = x_ref[...] + mod_ref[0, 5:6, :] * y
    _ffn_body(x, mod_ref, g_ref, wg_ref, wu_ref, wd_ref, o_ref, idx, fc)


def _ffn_pool_kernel(x_ref, u_ref, wp_ref, ps_ref, mod_ref, g_ref, wg_ref, wu_ref, wd_ref, o_ref, *, idx, fc):
    gc = wp_ref.shape[1]
    ys = [jnp.dot(u_ref[:, k * gc:(k + 1) * gc], wp_ref[k], preferred_element_type=F32)
          for k in range(wp_ref.shape[0])]
    y = jnp.concatenate(ys, axis=-1) * ps_ref[...]
    x = x_ref[...] + mod_ref[0, 5:6, :] * y
    _ffn_body(x, mod_ref, g_ref, wg_ref, wu_ref, wd_ref, o_ref, idx, fc)


def _ffn_call(x, mod, g, wg, wu, wd, *, layer, half, idx, tiles_per_mod, mod_base, tm, mixer=None):
    n, d = x.shape
    d_ff = wg.shape[3]
    fc = MXU_DIM
    tok = pl.BlockSpec((tm, d), lambda i: (i, 0))
    mod_spec = pl.BlockSpec((1, N_MOD, d), lambda i: (mod_base + i // tiles_per_mod, 0, 0))
    wspec = lambda r, c: pl.BlockSpec((None, None, r, c), lambda i: (layer, half, 0, 0),
                                      pipeline_mode=pl.Buffered(1))
    common_specs = [mod_spec, _resident((1, d)), wspec(d, d_ff), wspec(d, d_ff), wspec(d_ff, d)]
    common_args = [mod, g.reshape(1, d), wg, wu, wd]
    if mixer is None:
        kern = functools.partial(_ffn_kernel, idx=idx, fc=fc)
        specs, args = [tok] + common_specs, [x] + common_args
    elif mixer[0] == "attn":
        _, ud, um, wo = mixer
        kern = functools.partial(_ffn_attn_kernel, idx=idx, fc=fc)
        specs = [tok, pl.BlockSpec((tm, ud.shape[1]), lambda i: (i, 0)),
                 pl.BlockSpec((tm, um.shape[1]), lambda i: (i, 0)), _resident(wo.shape)] + common_specs
        args = [x, ud, um, wo] + common_args
    else:
        _, u, wp, ps = mixer
        kern = functools.partial(_ffn_pool_kernel, idx=idx, fc=fc)
        specs = [tok, pl.BlockSpec((tm, u.shape[1]), lambda i: (i, 0)), _resident(wp.shape),
                 _resident((1, d))] + common_specs
        args = [x, u, wp, ps.reshape(1, d)] + common_args
    return pl.pallas_call(
        kern,
        grid=(n // tm,),
        in_specs=specs,
        out_specs=tok,
        out_shape=jax.ShapeDtypeStruct((n, d), F32),
        compiler_params=_cparams(("parallel",)),
        name="ffn_half",
    )(*args)


def _group_rms(x, gm_ref, gain):
    outs = []
    for j in range(x.shape[1] // MXU_DIM):
        xb = x[:, j * MXU_DIM:(j + 1) * MXU_DIM]
        ms = jnp.dot((xb * xb).astype(BF16), gm_ref[...], preferred_element_type=F32)
        outs.append(xb * lax.rsqrt(ms + EPS) * gain[:, j * MXU_DIM:(j + 1) * MXU_DIM])
    return jnp.concatenate(outs, axis=-1)


def _rope(x, cos, sin, half):
    outs = []
    lane = lax.broadcasted_iota(jnp.int32, (x.shape[0], LANES), 1)
    first = (lane % (2 * half)) < half
    for j in range(x.shape[1] // LANES):
        xb = x[:, j * LANES:(j + 1) * LANES]
        partner = jnp.where(first, pltpu.roll(xb, LANES - half, 1), pltpu.roll(xb, half, 1))
        outs.append(xb * cos + partner * sin)
    return jnp.concatenate(outs, axis=-1)


def _store_values_t(v, vt_ref):
    for j in range(v.shape[1] // LANES):
        t = v[:, j * LANES:(j + 1) * LANES].T.astype(BF16)
        for c in range(vt_ref.shape[1]):
            vt_ref[0, c, j * LANES:(j + 1) * LANES, :] = t[:, c * KEY_CHUNK:(c + 1) * KEY_CHUNK]


def _proj_kernel(x_ref, mod_ref, g_ref, win_ref, wqb_ref, wkvb_ref, g64_ref, gm_ref,
                 gqd_ref, gkd_ref, gqa_ref, gkva_ref, gqm_ref, gkm_ref, gkr_ref,
                 cd_ref, sd_ref, cm_ref, sm_ref, *refs):
    qdt_ref, qmt_ref, kd_ref, km_ref, vdt_ref, vmt_ref = refs[-6:]
    x = x_ref[...]
    h = _modulate(x, g_ref[...], mod_ref[0, 3:4, :], mod_ref[0, 4:5, :]).astype(BF16)

    def inp(lo, hi):
        return jnp.dot(h, win_ref[:, lo:hi], preferred_element_type=F32)

    cd, sd, cm, sm = cd_ref[...], sd_ref[...], cm_ref[...], sm_ref[...]

    dq = _rope(_group_rms(inp(IN_DQ, IN_DK), g64_ref, gqd_ref[...]), cd, sd, DIFF_HD // 4)
    row = lax.broadcasted_iota(jnp.int32, (LANES, x.shape[0]), 0)
    for hd in range(DIFF_HEADS):
        t = dq[:, hd * LANES:(hd + 1) * LANES].T
        qdt_ref[0, (2 * hd) * LANES:(2 * hd + 1) * LANES, :] = jnp.where(row < DIFF_HD, t, 0.0).astype(BF16)
        qdt_ref[0, (2 * hd + 1) * LANES:(2 * hd + 2) * LANES, :] = jnp.where(row < DIFF_HD, 0.0, t).astype(BF16)
    dk = _rope(_group_rms(inp(IN_DK, IN_DV), g64_ref, gkd_ref[...]), cd, sd, DIFF_HD // 4)
    kd_ref[0] = dk.astype(BF16)
    _store_values_t(inp(IN_DV, IN_CQ), vdt_ref)

    cq = inp(IN_CQ, IN_CKV)
    cq = cq * lax.rsqrt(jnp.mean(cq * cq, axis=-1, keepdims=True) + EPS) * gqa_ref[...]
    mq = jnp.dot(cq.astype(BF16), wqb_ref[...], preferred_element_type=F32)
    mq = _rope(_group_rms(mq, gm_ref, gqm_ref[...]), cm, sm, MLA_ROPE // 4)
    for hd in range(MLA_HEADS):
        qmt_ref[0, hd * LANES:(hd + 1) * LANES, :] = mq[:, hd * LANES:(hd + 1) * LANES].T.astype(BF16)

    ckv = inp(IN_CKV, IN_KR)
    ckv = ckv * lax.rsqrt(jnp.mean(ckv * ckv, axis=-1, keepdims=True) + EPS) * gkva_ref[...]
    kv = jnp.dot(ckv.astype(BF16), wkvb_ref[...], preferred_element_type=F32)
    n_k = MLA_HEADS * LANES
    kn = _group_rms(kv[:, :n_k], gm_ref, gkm_ref[...])
    _store_values_t(kv[:, n_k:], vmt_ref)

    kr_raw = inp(IN_KR, IN_W_PAD)
    kr2 = jnp.concatenate([kr_raw, kr_raw], axis=-1)
    kr = _group_rms(kr2, gm_ref, jnp.concatenate([gkr_ref[...], gkr_ref[...]], axis=-1))[:, :LANES]
    kr = _rope(kr, cm, sm, MLA_ROPE // 4)
    km_ref[0] = (kn + jnp.concatenate([kr] * MLA_HEADS, axis=-1)).astype(BF16)


def _proj_call(x, mod, g, pw, tables, kv_bufs, *, seq, key_len, key_off, tiles_per_mod, mod_base, tm):
    n, d = x.shape
    b = n // seq
    tps = seq // tm
    cpt = tm // KEY_CHUNK
    assert key_off % tm == 0 and tm % KEY_CHUNK == 0
    tab = pl.BlockSpec((tm, LANES), lambda i: (i % tps, 0))
    mod_spec = pl.BlockSpec((1, N_MOD, d), lambda i: (mod_base + i // tiles_per_mod, 0, 0))
    consts = [pw["w_in"], pw["w_q_b"], pw["w_kv_b"], pw["g64"], pw["gm"],
              pw["gqd"], pw["gkd"], pw["gqa"], pw["gkva"], pw["gqm"], pw["gkm"], pw["gkr"]]
    qrows = 2 * DIFF_HEADS * LANES
    qspec = pl.BlockSpec((1, qrows, tm), lambda i: (i // tps, 0, i % tps))
    kspec = lambda w: pl.BlockSpec((1, tm, w), lambda i: (i // tps, key_off // tm + i % tps, 0))
    vspec = pl.BlockSpec((1, cpt, 512, KEY_CHUNK), lambda i: (i // tps, key_off // tm + i % tps, 0, 0))
    nkc = key_len // KEY_CHUNK
    out_shape = [jax.ShapeDtypeStruct((b, qrows, seq), BF16), jax.ShapeDtypeStruct((b, qrows, seq), BF16),
                 jax.ShapeDtypeStruct((b, key_len, 512), BF16), jax.ShapeDtypeStruct((b, key_len, qrows), BF16),
                 jax.ShapeDtypeStruct((b, nkc, 512, KEY_CHUNK), BF16),
                 jax.ShapeDtypeStruct((b, nkc, 512, KEY_CHUNK), BF16)]
    in_specs = ([pl.BlockSpec((tm, d), lambda i: (i, 0)), mod_spec, _resident((1, d))]
                + [_resident(c.shape) for c in consts] + [tab] * 4)
    args = [x, mod, g.reshape(1, d), *consts, *tables]
    aliases = {}
    if kv_bufs is not None:
        aliases = {len(args) + j: 2 + j for j in range(4)}
        in_specs += [pl.BlockSpec(memory_space=pl.ANY)] * 4
        args += list(kv_bufs)
    return pl.pallas_call(
        _proj_kernel,
        grid=(n // tm,),
        in_specs=in_specs,
        out_specs=[qspec, qspec, kspec(512), kspec(qrows), vspec, vspec],
        out_shape=out_shape,
        input_output_aliases=aliases,
        compiler_params=_cparams(("parallel",)),
        name="attn_proj",
    )(*args)


def _online_step(s, vt, m_ref, l_ref, acc_ref, k):
    m_old = m_ref[k:k + 1, :]
    m_new = jnp.maximum(m_old, jnp.max(s, axis=0, keepdims=True))
    alpha = jnp.exp2(m_old - m_new)
    p = jnp.exp2(s - m_new)
    l_ref[k:k + 1, :] = alpha * l_ref[k:k + 1, :] + jnp.sum(p, axis=0, keepdims=True)
    acc_ref[k] = alpha * acc_ref[k] + jnp.dot(vt, p.astype(BF16), preferred_element_type=F32)
    m_ref[k:k + 1, :] = m_new


def _init_stats(m_ref, l_ref, acc_ref):
    m_ref[...] = jnp.full(m_ref.shape, -jnp.inf, F32)
    l_ref[...] = jnp.zeros(l_ref.shape, F32)
    acc_ref[...] = jnp.zeros(acc_ref.shape, F32)


def _diff_finalize(lam_ref, sg_ref, o_ref, l_ref, acc_ref, lam_init):
    dl = lam_ref[...]
    lam = (jnp.exp(jnp.sum(dl[0:1] * dl[1:2], keepdims=True))
           - jnp.exp(jnp.sum(dl[2:3] * dl[3:4], keepdims=True)) + lam_init)
    for hd in range(DIFF_HEADS):
        o1 = acc_ref[2 * hd] / l_ref[2 * hd:2 * hd + 1, :]
        o2 = acc_ref[2 * hd + 1] / l_ref[2 * hd + 1:2 * hd + 2, :]
        o = o1 - lam * o2
        r = lax.rsqrt(jnp.mean(o * o, axis=0, keepdims=True) + EPS)
        o = (o * r) * sg_ref[...] * (1.0 - lam_init)
        o_ref[0, :, hd * DIFF_VD:(hd + 1) * DIFF_VD] = o.T.astype(BF16)


def _mla_finalize(o_ref, l_ref, acc_ref):
    for pair in range(MLA_HEADS // 2):
        o = jnp.concatenate([acc_ref[2 * pair + j] / l_ref[2 * pair + j:2 * pair + j + 1, :] for j in range(2)],
                            axis=0)
        o_ref[0, :, pair * LANES:(pair + 1) * LANES] = o.T.astype(BF16)


def _online_kernel(qt_ref, k_ref, vt_ref, *refs, n_heads, heads_per_kv, dv, finalize):
    m_ref, l_ref, acc_ref = refs[-3:]
    j = pl.program_id(2)

    @pl.when(j == 0)
    def _():
        _init_stats(m_ref, l_ref, acc_ref)

    for k in range(n_heads):
        kv = k // heads_per_kv
        s = jnp.dot(k_ref[0, :, kv * LANES:(kv + 1) * LANES], qt_ref[0, k * LANES:(k + 1) * LANES, :],
                    preferred_element_type=F32)
        _online_step(s, vt_ref[0, 0, kv * dv:(kv + 1) * dv, :], m_ref, l_ref, acc_ref, k)

    @pl.when(j == pl.num_programs(2) - 1)
    def _():
        finalize(*refs[:-3], l_ref, acc_ref)


def _bounded_kernel(qt_ref, k_ref, vt_ref, *refs, n_heads, heads_per_kv, dv, cpi, qk_lag, finalize):
    l_ref, acc_ref = refs[-2:]
    l_ref[...] = jnp.zeros(l_ref.shape, F32)
    acc_ref[...] = jnp.zeros(acc_ref.shape, F32)
    rows = cpi * KEY_CHUNK
    lag = cpi if qk_lag is None else min(qk_lag, cpi)

    def body(it, carry):
        r0 = pl.multiple_of(it * rows, KEY_CHUNK)
        for k in range(n_heads):
            kv = k // heads_per_kv
            qt = qt_ref[0, k * LANES:(k + 1) * LANES, :]

            def keys(lo, n):
                return k_ref[0, pl.ds(r0 + lo * KEY_CHUNK, n * KEY_CHUNK), kv * LANES:(kv + 1) * LANES]

            s, pv, ls = {}, None, None
            if qk_lag is None:
                whole = jnp.dot(keys(0, cpi), qt, preferred_element_type=F32)
                s = {u: whole[u * KEY_CHUNK:(u + 1) * KEY_CHUNK] for u in range(cpi)}
            for u in range(cpi + lag):
                if u < cpi and qk_lag is not None:
                    s[u] = jnp.dot(keys(u, 1), qt, preferred_element_type=F32)
                if u >= lag:
                    p = jnp.exp2(s.pop(u - lag))
                    psum = jnp.sum(p, axis=0, keepdims=True)
                    d = jnp.dot(vt_ref[0, it * cpi + u - lag, kv * dv:(kv + 1) * dv, :], p.astype(BF16),
                                preferred_element_type=F32)
                    pv = d if pv is None else pv + d
                    ls = psum if ls is None else ls + psum
            l_ref[k:k + 1, :] += ls
            acc_ref[k] += pv
        return carry

    lax.fori_loop(0, vt_ref.shape[1] // cpi, body, 0)
    finalize(*refs[:-2], l_ref, acc_ref)


def _attn_call(kind, bounded, qt, k, vt, extra, lam_init, *, tq, key_block, n_keys, cpi):
    b, _, sq = qt.shape
    nkc = n_keys // KEY_CHUNK
    if kind == "diff":
        dims = dict(n_heads=2 * DIFF_HEADS, heads_per_kv=2, dv=DIFF_VD,
                    finalize=functools.partial(_diff_finalize, lam_init=lam_init))
        extra_specs = [pl.BlockSpec(extra[0].shape, lambda *_: (0, 0)), pl.BlockSpec((DIFF_VD, 1), lambda *_: (0, 0))]
    else:
        dims = dict(n_heads=MLA_HEADS, heads_per_kv=1, dv=MLA_V, finalize=_mla_finalize)
        extra_specs = []
    n_heads, dv = dims["n_heads"], dims["dv"]
    stats = [pltpu.VMEM((n_heads, tq), F32)] * (1 if bounded else 2)
    scratch = stats + [pltpu.VMEM((n_heads, dv, tq), F32)]
    vrows = vt.shape[2]
    out_shape = jax.ShapeDtypeStruct((b, sq, vrows), BF16)
    if bounded:
        body = functools.partial(_bounded_kernel, cpi=cpi, qk_lag=None if kind == "diff" else MLA_QK_LAG, **dims)
        grid = (b, sq // tq)
        k_bufs = 2 if k.shape[2] * n_keys * 2 <= KV_DOUBLE_BUFFER_BYTES else 1
        in_specs = [
            pl.BlockSpec((1, qt.shape[1], tq), lambda b_, i: (b_, 0, i)),
            pl.BlockSpec((1, n_keys, k.shape[2]), lambda b_, i: (b_, key_block, 0),
                         pipeline_mode=pl.Buffered(k_bufs)),
            pl.BlockSpec((1, nkc, vrows, KEY_CHUNK), lambda b_, i: (b_, key_block, 0, 0)),
        ]
        out_spec = pl.BlockSpec((1, tq, vrows), lambda b_, i: (b_, i, 0))
        sem = ("parallel", "arbitrary")
    else:
        body = functools.partial(_online_kernel, **dims)
        grid = (b, sq // tq, nkc)
        in_specs = [
            pl.BlockSpec((1, qt.shape[1], tq), lambda b_, i, j: (b_, 0, i)),
            pl.BlockSpec((1, KEY_CHUNK, k.shape[2]), lambda b_, i, j: (b_, key_block * nkc + j, 0)),
            pl.BlockSpec((1, 1, vrows, KEY_CHUNK), lambda b_, i, j: (b_, key_block * nkc + j, 0, 0)),
        ]
        out_spec = pl.BlockSpec((1, tq, vrows), lambda b_, i, j: (b_, i, 0))
        sem = ("parallel", "parallel", "arbitrary")
    return pl.pallas_call(
        body, grid=grid, in_specs=in_specs + extra_specs, out_specs=out_spec, out_shape=out_shape,
        scratch_shapes=scratch, compiler_params=_cparams(sem),
        name=kind + ("_attn_bounded" if bounded else "_attn"),
    )(qt, k, vt, *extra)


HALO = 8
POOL_ROW_BLOCK = 64


def _pool_kernel(xp_ref, x_ref, xn_ref, mod_ref, g_ref, o_ref, hc_ref, *, tm, seq):
    i = pl.program_id(1)
    g = g_ref[...]
    shift, scale = mod_ref[0, 3:4, :], mod_ref[0, 4:5, :]
    hc_ref[0:HALO, :] = jnp.where(i > 0, _modulate(xp_ref[0], g, shift, scale), 0.0)
    hc_ref[HALO:HALO + tm, :] = _modulate(x_ref[0], g, shift, scale)
    hc_ref[HALO + tm:, :] = jnp.where(i < pl.num_programs(1) - 1, _modulate(xn_ref[0], g, shift, scale), 0.0)
    gc = x_ref.shape[2] // len(POOL_WINDOWS)
    rb = POOL_ROW_BLOCK
    for r0 in range(0, tm, rb):
        t = i * tm + r0 + lax.broadcasted_iota(jnp.int32, (rb, 1), 0)
        for k, w in enumerate(POOL_WINDOWS):
            cols = slice(k * gc, (k + 1) * gc)
            lo = HALO + r0 - w // 2
            acc = hc_ref[lo:lo + rb, cols]
            for s in range(1, w):
                acc = acc + hc_ref[lo + s:lo + s + rb, cols]
            cnt = (jnp.minimum(t + w // 2, seq) - jnp.maximum(t - w // 2, 0)).astype(F32)
            o_ref[0, r0:r0 + rb, cols] = (acc / cnt - hc_ref[HALO + r0:HALO + r0 + rb, cols]).astype(BF16)


def _pool_call(x, mod, g, *, mod_base, mod_per_batch, tm):
    b, seq, d = x.shape
    nb = tm // HALO
    last = seq // HALO - 1
    mod_spec = pl.BlockSpec((1, N_MOD, d), lambda b_, i: (mod_base + b_ * mod_per_batch, 0, 0))
    return pl.pallas_call(
        functools.partial(_pool_kernel, tm=tm, seq=seq),
        grid=(b, seq // tm),
        in_specs=[
            pl.BlockSpec((1, HALO, d), lambda b_, i: (b_, jnp.maximum(i * nb - 1, 0), 0)),
            pl.BlockSpec((1, tm, d), lambda b_, i: (b_, i, 0)),
            pl.BlockSpec((1, HALO, d), lambda b_, i: (b_, jnp.minimum((i + 1) * nb, last), 0)),
            mod_spec,
            pl.BlockSpec((1, d), lambda b_, i: (0, 0)),
        ],
        out_specs=pl.BlockSpec((1, tm, d), lambda b_, i: (b_, i, 0)),
        out_shape=jax.ShapeDtypeStruct((b, seq, d), BF16),
        scratch_shapes=[pltpu.VMEM((tm + 2 * HALO, d), F32)],
        compiler_params=_cparams(("parallel", "parallel")),
        name="pool_mixer",
    )(x, x, x, mod, g.reshape(1, d))


def _rope_tables(seq, dim, pad_lo, pad_hi, reps):
    q = dim // 4
    t = jnp.arange(seq)
    freqs = ROPE_BASE ** (-jnp.arange(q, dtype=F32) / q)
    ar = (t // GRID_W).astype(F32)[:, None] * freqs
    ac = (t % GRID_W).astype(F32)[:, None] * freqs
    cos = jnp.concatenate([jnp.cos(ar), jnp.cos(ar), jnp.cos(ac), jnp.cos(ac)], axis=-1)
    sin = jnp.concatenate([-jnp.sin(ar), jnp.sin(ar), -jnp.sin(ac), jnp.sin(ac)], axis=-1)
    one = lambda w: jnp.ones((seq, w), F32)
    zero = lambda w: jnp.zeros((seq, w), F32)
    cos = jnp.tile(jnp.concatenate([one(pad_lo), cos, one(pad_hi)], axis=-1), (1, reps))
    sin = jnp.tile(jnp.concatenate([zero(pad_lo), sin, zero(pad_hi)], axis=-1), (1, reps))
    return cos, sin


def _group_matrix(sizes):
    m = jnp.zeros((MXU_DIM, MXU_DIM), F32)
    lo = 0
    while lo < MXU_DIM:
        for size, width in sizes:
            if size:
                m = m.at[lo:lo + width, lo:lo + width].set(1.0 / size)
            lo += width
    return m.astype(BF16)


def _attn_params(w_in, qk_g, q_a_g, w_q_b, kv_a_g, w_kv_b, nope_g, rope_g):
    d = w_in.shape[0]
    hq = MLA_NOPE + MLA_ROPE
    pad = LANES - hq
    w_in_p = jnp.concatenate([w_in[:, :IN_KR], jnp.zeros((d, MLA_NOPE), F32), w_in[:, IN_KR:],
                              jnp.zeros((d, pad), F32)], axis=-1)
    wq = w_q_b.reshape(MLA_Q_RANK, MLA_HEADS, hq)
    wq = jnp.concatenate([wq, jnp.zeros((MLA_Q_RANK, MLA_HEADS, pad), F32)], axis=-1)
    wkv = w_kv_b.reshape(MLA_KV_RANK, MLA_HEADS, MLA_NOPE + MLA_V)
    wk = jnp.concatenate([wkv[..., :MLA_NOPE], jnp.zeros((MLA_KV_RANK, MLA_HEADS, LANES - MLA_NOPE), F32)], axis=-1)
    wv = wkv[..., MLA_NOPE:]
    q_scale = hq ** -0.5 * LOG2E
    zeros = lambda w: jnp.zeros((w,), F32)
    gqm = jnp.tile(jnp.concatenate([nope_g[0], rope_g[0], zeros(pad)]), MLA_HEADS) * q_scale
    gkm = jnp.tile(jnp.concatenate([nope_g[1], zeros(LANES - MLA_NOPE)]), MLA_HEADS)
    gkr = jnp.concatenate([zeros(MLA_NOPE), rope_g[1], zeros(pad)])
    row = lambda v: v.reshape(1, -1)
    return {
        "w_in": w_in_p.astype(BF16),
        "w_q_b": wq.reshape(MLA_Q_RANK, MLA_HEADS * LANES).astype(BF16),
        "w_kv_b": jnp.concatenate([wk.reshape(MLA_KV_RANK, -1), wv.reshape(MLA_KV_RANK, -1)], axis=-1).astype(BF16),
        "g64": _group_matrix([(DIFF_HD, DIFF_HD)]),
        "gm": _group_matrix([(MLA_NOPE, MLA_NOPE), (MLA_ROPE, MLA_ROPE), (0, pad)]),
        "gqd": row(jnp.tile(qk_g[0], 2 * DIFF_HEADS) * (DIFF_HD ** -0.5 * LOG2E)),
        "gkd": row(jnp.tile(qk_g[1], 2 * DIFF_HEADS)),
        "gqa": row(q_a_g), "gkva": row(kv_a_g),
        "gqm": row(gqm), "gkm": row(gkm), "gkr": row(gkr),
    }


def _logit_bounds(qk_g, nope_g, rope_g):
    amax = lambda v: jnp.max(jnp.abs(v))
    diff = LOG2E * DIFF_HD ** 0.5 * amax(qk_g[0]) * amax(qk_g[1])
    qn = jnp.sqrt(MLA_NOPE * amax(nope_g[0]) ** 2 + MLA_ROPE * amax(rope_g[0]) ** 2)
    kn = jnp.sqrt(MLA_NOPE * amax(nope_g[1]) ** 2 + MLA_ROPE * amax(rope_g[1]) ** 2)
    mla = LOG2E * (MLA_NOPE + MLA_ROPE) ** -0.5 * qn * kn
    return diff, mla


def _attention(qdt, qmt, kv, dlam, subln_g, lam_init, bounds, *, tq, key_block, n_keys, cpi):
    kd, km, vdt, vmt = kv

    def core(kind, bound, qt, k, vt, extra):
        call = functools.partial(_attn_call, kind, qt=qt, k=k, vt=vt, extra=extra, lam_init=lam_init,
                                 tq=tq, key_block=key_block, n_keys=n_keys, cpi=cpi)
        return lax.cond(bound <= MAX_SAFE_LOGIT, lambda: call(bounded=True), lambda: call(bounded=False))

    od = core("diff", bounds[0], qdt, kd, vdt, (dlam, subln_g.reshape(DIFF_VD, 1)))
    om = core("mla", bounds[1], qmt, km, vmt, ())
    return od, om


def _largest_tile(n, cap, mult):
    best = None
    for t in range(mult, min(n, cap) + 1, mult):
        if n % t == 0:
            best = t
    assert best is not None, (n, cap, mult)
    return best


def kernel(x, c, ctx, c_ctx, mod_w, mod_b, norm_g, ffn_w_gate, ffn_w_up, ffn_w_down, attn_w_in, diff_qk_g,
           diff_lambda, diff_subln_g, mla_q_a_g, mla_w_q_b, mla_kv_a_g, mla_w_kv_b, mla_nope_g, mla_rope_g,
           attn_w_out, pool_w, pool_scale):
    b, s, d = x.shape
    lc = ctx.shape[1]
    depth = mod_w.shape[0]
    assert s % GRID_W == 0 and d == 1024
    assert lc % KEY_CHUNK == 0 and s % lc == 0

    tm_l = _largest_tile(s, 512, KEY_CHUNK)
    tm_c = _largest_tile(lc, 512, KEY_CHUNK)
    tq = _largest_tile(s, 512, 128)
    cpi = _largest_tile((s + lc) // KEY_CHUNK, 33, 1)

    rows = -(-(b + 1) // 8) * 8
    cvec = jnp.concatenate([c, c_ctx[None], jnp.zeros((rows - b - 1, d), F32)], axis=0)
    mods = _mod_call(cvec, mod_w, mod_b).reshape(depth, rows, N_MOD, d)

    wg, wu, wd = ffn_w_gate.astype(BF16), ffn_w_up.astype(BF16), ffn_w_down.astype(BF16)
    cos_d, sin_d = _rope_tables(s, DIFF_HD, 0, 0, LANES // DIFF_HD)
    cos_m, sin_m = _rope_tables(s, MLA_ROPE, MLA_NOPE, LANES - MLA_NOPE - MLA_ROPE, 1)
    tab_l = (cos_d, sin_d, cos_m, sin_m)
    ones, zeros = jnp.ones((lc, LANES), F32), jnp.zeros((lc, LANES), F32)
    tab_c = (ones, zeros, ones, zeros)

    xl = x.reshape(b * s, d)
    xc = ctx.reshape(b * lc, d)
    lat = dict(tiles_per_mod=s // tm_l, mod_base=0, tm=tm_l)
    cx = dict(tiles_per_mod=b * lc // tm_c + 1, mod_base=b, tm=tm_c)

    for layer in range(depth):
        even = layer % 2 == 0
        ctx_out = layer < depth - 1
        ctx_in = ctx_out or even
        i = layer // 2
        g = norm_g[layer]
        mod = mods[layer]
        ffn = functools.partial(_ffn_call, wg=wg, wu=wu, wd=wd, layer=layer)

        xl = ffn(xl, mod, g[0], half=0, idx=0, **lat)
        if ctx_in:
            xc = ffn(xc, mod, g[0], half=0, idx=0, **cx)

        if even:
            lam_init = 0.8 - 0.6 * math.exp(-0.3 * layer)
            pw = _attn_params(attn_w_in[i], diff_qk_g[i], mla_q_a_g[i], mla_w_q_b[i], mla_kv_a_g[i],
                              mla_w_kv_b[i], mla_nope_g[i], mla_rope_g[i])
            pl_ = _proj_call(xl, mod, g[1], pw, tab_l, None, seq=s, key_len=s + lc, key_off=0, **lat)
            pc_ = _proj_call(xc, mod, g[1], pw, tab_c, pl_[2:], seq=lc, key_len=s + lc, key_off=s, **cx)
            kv = pc_[2:]
            bounds = _logit_bounds(diff_qk_g[i], mla_nope_g[i], mla_rope_g[i])
            attn = functools.partial(_attention, kv=kv, dlam=diff_lambda[i].astype(F32), subln_g=diff_subln_g[i],
                                     lam_init=lam_init, bounds=bounds)
            od, om = attn(pl_[0], pl_[1], tq=tq, key_block=0, n_keys=s + lc, cpi=cpi)
            wo = attn_w_out[i].astype(BF16)
            mix_l = ("attn", od.reshape(b * s, -1), om.reshape(b * s, -1), wo)
            mix_c = None
            if ctx_out:
                od, om = attn(pc_[0], pc_[1], tq=lc, key_block=s // lc, n_keys=lc, cpi=lc // KEY_CHUNK)
                mix_c = ("attn", od.reshape(b * lc, -1), om.reshape(b * lc, -1), wo)
        else:
            wp = pool_w[i].astype(BF16)
            dl_ = _pool_call(xl.reshape(b, s, d), mod, g[1], mod_base=0, mod_per_batch=1, tm=tm_l)
            mix_l = ("pool", dl_.reshape(b * s, d), wp, pool_scale[i])
            mix_c = None
            if ctx_out:
                dc_ = _pool_call(xc.reshape(b, lc, d), mod, g[1], mod_base=b, mod_per_batch=0, tm=tm_c)
                mix_c = ("pool", dc_.reshape(b * lc, d), wp, pool_scale[i])

        xl = ffn(xl, mod, g[2], half=1, idx=2, mixer=mix_l, **lat)
        if ctx_out:
            xc = ffn(xc, mod, g[2], half=1, idx=2, mixer=mix_c, **cx)

    return xl.reshape(b, s, d)
```
